```python
import math
import jax
import jax.numpy as jnp
from jax import lax

D_MODEL = 2048
BATCH = 4
SEQ = 4096
DEPTH = 4

N_MIXERS = 2
N_HEADS = 16
HEAD_DIM = D_MODEL // N_HEADS
MOBA_BLOCK = 256
MOBA_TOPK = 3
Q_CHUNK = 16
REL_BUCKETS = 32
REL_MAX_DIST = 2048
CONV_K = 31
N_EXPERTS = 64
EXPERT_FF = 512
TOP_K = 8
N_GROUPS = 8
TOPK_GROUPS = 4
ROUTED_SCALE = 2.5
DISPATCH_ROWS = 512
N_ATTN_LAYERS = (DEPTH + 1) // 2
N_CONV_LAYERS = DEPTH // 2
DN_ALPHA = (2 * DEPTH) ** 0.25
DN_BETA = (8 * DEPTH) ** -0.25
LN_EPS = 1e-5

kernel_name = "hybrid_moba_conformer_moe_deepnorm"


def layer_norm(x, g, b):
    xf = x.astype(jnp.float32)
    mu = jnp.mean(xf, -1, keepdims=True)
    var = jnp.mean(jnp.square(xf - mu), -1, keepdims=True)
    return ((xf - mu) * lax.rsqrt(var + LN_EPS)).astype(x.dtype) * g + b


def t5_bucket(rel):
    n = jnp.maximum(rel, 0)
    max_exact = REL_BUCKETS // 2
    nf = jnp.maximum(n, 1).astype(jnp.float32)
    large = max_exact + (jnp.log(nf / max_exact) / math.log(REL_MAX_DIST / max_exact)
                         * (REL_BUCKETS - max_exact)).astype(jnp.int32)
    large = jnp.minimum(large, REL_BUCKETS - 1)
    return jnp.where(n < max_exact, n, large)


def moba_attention(x, w_qkv, w_o, rel_bias):
    B, S, D = x.shape
    H, dh = N_HEADS, HEAD_DIM
    q, k, v = jnp.split(x @ w_qkv, 3, axis=-1)
    heads = lambda t: t.reshape(B, S, H, dh).transpose(0, 2, 1, 3)
    q, k, v = heads(q), heads(k), heads(v)
    nb = -(-S // MOBA_BLOCK)
    pad = nb * MOBA_BLOCK - S
    k_p = jnp.pad(k, ((0, 0), (0, 0), (0, pad), (0, 0)))
    v_p = jnp.pad(v, ((0, 0), (0, 0), (0, pad), (0, 0)))
    kb = k_p.reshape(B, H, nb, MOBA_BLOCK, dh)
    vb = v_p.reshape(B, H, nb, MOBA_BLOCK, dh)
    k_mean = jnp.mean(kb, axis=3)
    topk = min(MOBA_TOPK, nb)
    scale = dh ** -0.5
    bias_t = rel_bias.T
    b_ix = jnp.arange(B)[:, None, None, None]
    h_ix = jnp.arange(H)[None, :, None, None]
    blk_ids = jnp.arange(nb)
    offs = jnp.arange(MOBA_BLOCK)

    def chunk(c):
        q0 = c * Q_CHUNK
        qc = lax.dynamic_slice_in_dim(q, q0, Q_CHUNK, axis=2)
        q_pos = q0 + jnp.arange(Q_CHUNK)
        j = q0 // MOBA_BLOCK
        gate = jnp.einsum('bhcd,bhnd->bhcn', qc, k_mean).astype(jnp.float32)
        gate = jnp.where(blk_ids < j, gate, -jnp.inf)
        _, sel = lax.top_k(gate, topk)
        sel_valid = jnp.arange(topk) < j
        k_sel = kb[b_ix, h_ix, sel]
        v_sel = vb[b_ix, h_ix, sel]
        k_pos_sel = sel[..., None] * MOBA_BLOCK + offs
        s_sel = jnp.einsum('bhcd,bhctkd->bhctk', qc, k_sel) * scale
        s_sel = s_sel + bias_t[h_ix[..., None], t5_bucket(q_pos[:, None, None] - k_pos_sel)]
        s_sel = jnp.where(sel_valid[:, None], s_sel, -jnp.inf)
        k_own = lax.dynamic_slice_in_dim(k_p, j * MOBA_BLOCK, MOBA_BLOCK, axis=2)
        v_own = lax.dynamic_slice_in_dim(v_p, j * MOBA_BLOCK, MOBA_BLOCK, axis=2)
        rel_own = q_pos[:, None] - (j * MOBA_BLOCK + offs)[None, :]
        s_own = jnp.einsum('bhcd,bhkd->bhck', qc, k_own) * scale + bias_t[:, t5_bucket(rel_own)][None]
        s_own = jnp.where(rel_own >= 0, s_own, -jnp.inf)
        logits = jnp.concatenate([s_sel.reshape(B, H, Q_CHUNK, topk * MOBA_BLOCK), s_own], -1)
        p = jax.nn.softmax(logits.astype(jnp.float32), axis=-1).astype(v.dtype)
        p_sel = p[..., :topk * MOBA_BLOCK].reshape(B, H, Q_CHUNK, topk, MOBA_BLOCK)
        p_own = p[..., topk * MOBA_BLOCK:]
        return (jnp.einsum('bhctk,bhctkd->bhcd', p_sel, v_sel)
                + jnp.einsum('bhck,bhkd->bhcd', p_own, v_own))

    o = lax.map(chunk, jnp.arange(S // Q_CHUNK))
    o = o.transpose(1, 0, 3, 2, 4).reshape(B, S, H * dh)
    return o @ w_o


def conformer_conv(x, w_in, b_in, w_dw, b_dw, ln_g, ln_b, w_out, b_out):
    D = x.shape[-1]
    a, g = jnp.split(x @ w_in + b_in, 2, axis=-1)
    h = a * jax.nn.sigmoid(g)
    h = lax.conv_general_dilated(h, w_dw[:, None, :], window_strides=(1,),
                                 padding=[(CONV_K - 1, 0)],
                                 dimension_numbers=('NWC', 'WIO', 'NWC'),
                                 feature_group_count=D) + b_dw
    h = jax.nn.silu(layer_norm(h, ln_g, ln_b))
    return h @ w_out + b_out


def routed_experts(xf, idx, wts, w_gate, w_up, w_down):
    n_tok = xf.shape[0]
    n_exp = w_gate.shape[0]
    k = idx.shape[1]
    R = DISPATCH_ROWS
    n_asg = n_tok * k
    n_blocks = -(-n_asg // R)
    pad = n_blocks * R - n_asg
    e_flat = idx.reshape(-1).astype(jnp.int32)
    t_flat = jnp.repeat(jnp.arange(n_tok, dtype=jnp.int32), k)
    order = jnp.argsort(e_flat)
    s_e = jnp.concatenate([e_flat[order], jnp.full((pad,), n_exp, jnp.int32)])
    s_t = jnp.concatenate([t_flat[order], jnp.zeros((pad,), jnp.int32)])
    s_w = jnp.concatenate([wts.reshape(-1)[order], jnp.zeros((pad,), wts.dtype)])
    counts = jnp.bincount(e_flat, length=n_exp).astype(jnp.int32)
    ends = jnp.cumsum(counts)
    starts = ends - counts
    first_blk = starts // R
    n_items_e = jnp.where(counts > 0, (ends - 1) // R - first_blk + 1, 0)
    item_end = jnp.cumsum(n_items_e)
    item_start = item_end - n_items_e
    n_items = n_blocks + n_exp - 1

    def step(out, i):
        e = jnp.searchsorted(item_end, i, side='right').astype(jnp.int32)
        live = e < n_exp
        e = jnp.minimum(e, n_exp - 1)
        blk = jnp.clip(first_blk[e] + i - item_start[e], 0, n_blocks - 1)
        rows_t = lax.dynamic_slice_in_dim(s_t, blk * R, R)
        rows_e = lax.dynamic_slice_in_dim(s_e, blk * R, R)
        rows_w = lax.dynamic_slice_in_dim(s_w, blk * R, R)
        xs = xf[rows_t]
        y = (jax.nn.silu(xs @ w_gate[e]) * (xs @ w_up[e])) @ w_down[e]
        coef = jnp.where((rows_e == e) & live, rows_w, jnp.zeros_like(rows_w))
        return out.at[rows_t].add(y * coef[:, None]), None

    out, _ = lax.scan(step, jnp.zeros_like(xf), jnp.arange(n_items, dtype=jnp.int32))
    return out


def moe_ffn(x, w_router, r_bias, w_gate, w_up, w_down, s_gate, s_up, s_down):
    B, S, D = x.shape
    xf = x.reshape(B * S, D)
    n = B * S
    scores = jax.nn.sigmoid((xf @ w_router).astype(jnp.float32))
    biased = scores + r_bias.astype(jnp.float32)
    per_g = N_EXPERTS // N_GROUPS
    grp_score = lax.top_k(biased.reshape(n, N_GROUPS, per_g), 2)[0].sum(-1)
    _, gidx = lax.top_k(grp_score, TOPK_GROUPS)
    gmask = jax.nn.one_hot(gidx, N_GROUPS, dtype=jnp.float32).sum(1) > 0
    emask = jnp.repeat(gmask, per_g, axis=1)
    _, idx = lax.top_k(jnp.where(emask, biased, -jnp.inf), TOP_K)
    w = jnp.take_along_axis(scores, idx, axis=1)
    w = (w / jnp.sum(w, -1, keepdims=True) * ROUTED_SCALE).astype(x.dtype)
    routed = routed_experts(xf, idx, w, w_gate, w_up, w_down)
    shared = (jax.nn.silu(xf @ s_gate) * (xf @ s_up)) @ s_down
    return (routed + shared).reshape(B, S, D)


def setup_inputs(seed: int = 0) -> dict:
    key = jax.random.key(seed)
    ks = jax.random.split(key, 24)
    D, F, E = D_MODEL, EXPERT_FF, N_EXPERTS
    NA, NC, L = N_ATTN_LAYERS, N_CONV_LAYERS, DEPTH
    nrm = lambda k, shape, s: jax.random.normal(k, shape, jnp.float32) * s
    return {
        "x": nrm(ks[0], (BATCH, SEQ, D), 1.0),
        "rel_bias": nrm(ks[1], (REL_BUCKETS, N_HEADS), 0.2),
        "attn_w_qkv": nrm(ks[2], (NA, D, 3 * D), D ** -0.5),
        "attn_w_o": nrm(ks[3], (NA, D, D), D ** -0.5 * DN_BETA),
        "conv_w_in": nrm(ks[4], (NC, D, 2 * D), D ** -0.5),
        "conv_b_in": nrm(ks[5], (NC, 2 * D), 0.02),
        "conv_w_dw": nrm(ks[6], (NC, CONV_K, D), CONV_K ** -0.5),
        "conv_b_dw": nrm(ks[7], (NC, D), 0.02),
        "conv_ln_g": 1.0 + nrm(ks[8], (NC, D), 0.02),
        "conv_ln_b": nrm(ks[9], (NC, D), 0.02),
        "conv_w_out": nrm(ks[10], (NC, D, D), D ** -0.5 * DN_BETA),
        "conv_b_out": nrm(ks[11], (NC, D), 0.02),
        "ln_mix_g": 1.0 + nrm(ks[12], (L, D), 0.02),
        "ln_mix_b": nrm(ks[13], (L, D), 0.02),
        "ln_ffn_g": 1.0 + nrm(ks[14], (L, D), 0.02),
        "ln_ffn_b": nrm(ks[15], (L, D), 0.02),
        "moe_w_router": nrm(ks[16], (L, D, E), D ** -0.5),
        "moe_router_bias": nrm(ks[17], (L, E), 0.01),
        "moe_w_gate": nrm(ks[18], (L, E, D, F), D ** -0.5),
        "moe_w_up": nrm(ks[19], (L, E, D, F), D ** -0.5),
        "moe_w_down": nrm(ks[20], (L, E, F, D), F ** -0.5 * DN_BETA),
        "shared_w_gate": nrm(ks[21], (L, D, F), D ** -0.5),
        "shared_w_up": nrm(ks[22], (L, D, F), D ** -0.5),
        "shared_w_down": nrm(ks[23], (L, F, D), F ** -0.5 * DN_BETA),
    }


def reference(x, rel_bias, attn_w_qkv, attn_w_o, conv_w_in, conv_b_in, conv_w_dw, conv_b_dw,
              conv_ln_g, conv_ln_b, conv_w_out, conv_b_out, ln_mix_g, ln_mix_b, ln_ffn_g, ln_ffn_b,
              moe_w_router, moe_router_bias, moe_w_gate, moe_w_up, moe_w_down,
              shared_w_gate, shared_w_up, shared_w_down):
    for i in range(DEPTH):
        m = i // N_MIXERS
        if i % N_MIXERS == 0:
            y = moba_attention(x, attn_w_qkv[m], attn_w_o[m], rel_bias)
        else:
            y = conformer_conv(x, conv_w_in[m], conv_b_in[m], conv_w_dw[m], conv_b_dw[m],
                               conv_ln_g[m], conv_ln_b[m], conv_w_out[m], conv_b_out[m])
        x = layer_norm(DN_ALPHA * x + y, ln_mix_g[i], ln_mix_b[i])
        y = moe_ffn(x, moe_w_router[i], moe_router_bias[i], moe_w_gate[i], moe_w_up[i],
                    moe_w_down[i], shared_w_gate[i], shared_w_up[i], shared_w_down[i])
        x = layer_norm(DN_ALPHA * x + y, ln_ffn_g[i], ln_ffn_b[i])
    return x
```

```python
import functools
import math

import jax
import jax.numpy as jnp
from jax import lax
from jax.experimental import pallas as pl
from jax.experimental.pallas import tpu as pltpu

N_HEADS = 16
MOBA_BLOCK = 256
MOBA_TOPK = 3
REL_BUCKETS = 32
REL_MAX_DIST = 2048
CONV_K = 31
TOP_K = 8
N_GROUPS = 8
TOPK_GROUPS = 4
ROUTED_SCALE = 2.5
DEPTH = 4
DN_ALPHA = (2 * DEPTH) ** 0.25
LN_EPS = 1e-5

LANES = 128
SUBLANES = 8
VMEM_LIMIT_BYTES = 56 * 1024 * 1024

NEG_BIG = -1e30

F32 = jnp.float32
BF16 = jnp.bfloat16


def _params(sem):
    return pltpu.CompilerParams(dimension_semantics=sem, vmem_limit_bytes=VMEM_LIMIT_BYTES)


def _pick(n, pref):
    t = min(pref, n)
    while n % t:
        t //= 2
    return t


def _layer_norm(z, g, b):
    mu = jnp.mean(z, axis=-1, keepdims=True)
    zc = z - mu
    var = jnp.mean(zc * zc, axis=-1, keepdims=True)
    return zc * lax.rsqrt(var + LN_EPS) * g + b


def _dot(a, b):
    return jnp.dot(a, b, preferred_element_type=F32)


def _dot_nt(a, b):
    return lax.dot_general(a, b, (((1,), (1,)), ((), ())), preferred_element_type=F32)


def _mm_kernel(x_ref, w_ref, o_ref):
    o_ref[...] = _dot(x_ref[...], w_ref[...]).astype(o_ref.dtype)


def matmul(x, w, out_dtype, tm=1024, tn=512):
    n, k = x.shape
    m = w.shape[1]
    tm, tn = _pick(n, tm), _pick(m, tn)
    return pl.pallas_call(
        _mm_kernel,
        grid=(n // tm, m // tn),
        in_specs=[pl.BlockSpec((tm, k), lambda i, j: (i, 0)),
                  pl.BlockSpec((k, tn), lambda i, j: (0, j))],
        out_specs=pl.BlockSpec((tm, tn), lambda i, j: (i, j)),
        out_shape=jax.ShapeDtypeStruct((n, m), out_dtype),
        compiler_params=_params(("parallel", "arbitrary")),
        name="matmul",
    )(x, w)


def _glu_kernel(x_ref, wa_ref, wg_ref, ba_ref, bg_ref, o_ref):
    x = x_ref[...]
    a = _dot(x, wa_ref[...]) + ba_ref[...]
    g = _dot(x, wg_ref[...]) + bg_ref[...]
    o_ref[...] = a * jax.nn.sigmoid(g)


def glu_proj(x, w, b, tm=1024, tn=256):
    n, k = x.shape
    d = w.shape[1] // 2
    tm, tn = _pick(n, tm), _pick(d, tn)
    nj = d // tn
    b2 = b.reshape(1, 2 * d)
    return pl.pallas_call(
        _glu_kernel,
        grid=(n // tm, nj),
        in_specs=[pl.BlockSpec((tm, k), lambda i, j: (i, 0)),
                  pl.BlockSpec((k, tn), lambda i, j: (0, j)),
                  pl.BlockSpec((k, tn), lambda i, j: (0, j + nj)),
                  pl.BlockSpec((1, tn), lambda i, j: (0, j)),
                  pl.BlockSpec((1, tn), lambda i, j: (0, j + nj))],
        out_specs=pl.BlockSpec((tm, tn), lambda i, j: (i, j)),
        out_shape=jax.ShapeDtypeStruct((n, d), F32),
        compiler_params=_params(("parallel", "arbitrary")),
        name="glu_proj",
    )(x, w, w, b2, b2)


def _proj_res_ln_kernel(a_ref, w_ref, bias_ref, res_ref, g_ref, b_ref, o_ref, ob_ref):
    y = _dot(a_ref[...], w_ref[...]) + bias_ref[...]
    z = _layer_norm(DN_ALPHA * res_ref[...] + y, g_ref[...], b_ref[...])
    o_ref[...] = z
    ob_ref[...] = z.astype(BF16)


def proj_res_ln(a, w, bias, res, g, b, tm=256):
    n, k = a.shape
    d = w.shape[1]
    tm = _pick(n, tm)
    row = lambda i: (i, 0)
    fix = lambda i: (0, 0)
    return pl.pallas_call(
        _proj_res_ln_kernel,
        grid=(n // tm,),
        in_specs=[pl.BlockSpec((tm, k), row), pl.BlockSpec((k, d), fix),
                  pl.BlockSpec((1, d), fix), pl.BlockSpec((tm, d), row),
                  pl.BlockSpec((1, d), fix), pl.BlockSpec((1, d), fix)],
        out_specs=[pl.BlockSpec((tm, d), row), pl.BlockSpec((tm, d), row)],
        out_shape=[jax.ShapeDtypeStruct((n, d), F32), jax.ShapeDtypeStruct((n, d), BF16)],
        compiler_params=_params(("parallel",)),
        name="proj_res_ln",
    )(a, w, bias.reshape(1, d), res, g.reshape(1, d), b.reshape(1, d))


def _t5_bucket(rel):
    n = jnp.maximum(rel, 0)
    max_exact = REL_BUCKETS // 2
    nf = jnp.maximum(n, 1).astype(F32)
    large = max_exact + (jnp.log(nf / max_exact) / math.log(REL_MAX_DIST / max_exact)
                         * (REL_BUCKETS - max_exact)).astype(jnp.int32)
    large = jnp.minimum(large, REL_BUCKETS - 1)
    return jnp.where(n < max_exact, n, large)


def _bias_tile_kernel(rb_ref, o_ref):
    h = pl.program_id(0)
    d = pl.program_id(1)
    blk = o_ref.shape[-1]
    row = lax.broadcasted_iota(jnp.int32, (blk, blk), 0)
    col = lax.broadcasted_iota(jnp.int32, (blk, blk), 1)
    bucket = _t5_bucket(d * blk + row - col)
    acc = jnp.zeros((blk, blk), F32)
    for k in range(REL_BUCKETS):
        acc = jnp.where(bucket == k, rb_ref[k, h], acc)
    o_ref[0, 0] = acc


def _num_bias_tiles(nb):
    last_start = (REL_MAX_DIST / (REL_BUCKETS // 2)) ** ((REL_BUCKETS // 2 - 1) / (REL_BUCKETS // 2)) \
        * (REL_BUCKETS // 2)
    d = 1
    while (d - 1) * MOBA_BLOCK + 1 < 1.05 * last_start + 1:
        d += 1
    return min(nb, d + 1)


def bias_tiles(rel_bias, n_heads, n_tiles):
    return pl.pallas_call(
        _bias_tile_kernel,
        grid=(n_heads, n_tiles),
        in_specs=[pl.BlockSpec(memory_space=pltpu.SMEM)],
        out_specs=pl.BlockSpec((1, 1, MOBA_BLOCK, MOBA_BLOCK), lambda h, d: (h, d, 0, 0)),
        out_shape=jax.ShapeDtypeStruct((n_heads, n_tiles, MOBA_BLOCK, MOBA_BLOCK), F32),
        compiler_params=_params(("parallel", "parallel")),
        name="bias_tiles",
    )(rel_bias)


def _moba_kernel(q_ref, k_ref, v_ref, bias_ref, o_ref, kmean_sc, *, nb, n_bias, scale):
    j = pl.program_id(2)
    blk = MOBA_BLOCK

    @pl.when(j == 0)
    def _():
        kmean_sc[...] = jnp.zeros_like(kmean_sc)
        for n in range(nb):
            kn = k_ref[0, n * blk:(n + 1) * blk, :].astype(F32)
            kmean_sc[n:n + 1, :] = jnp.mean(kn, axis=0, keepdims=True)

    q = q_ref[0]
    lane = lax.broadcasted_iota(jnp.int32, (blk, LANES), 1)

    gate = _dot_nt(q, kmean_sc[...].astype(BF16))
    gate = jnp.where(lane < j, gate, -jnp.inf)
    sel = jnp.zeros((blk, LANES), F32)
    for _ in range(MOBA_TOPK):
        mx = jnp.max(gate, axis=1, keepdims=True)
        first = jnp.min(jnp.where(gate == mx, lane, LANES), axis=1, keepdims=True)
        hit = lane == first
        sel = jnp.where(hit & (mx > -jnp.inf), 1.0, sel)
        gate = jnp.where(hit, -jnp.inf, gate)
    selneg = jnp.where(sel > 0.0, 0.0, NEG_BIG).astype(BF16)

    row = lax.broadcasted_iota(jnp.int32, (blk, blk), 0)
    col = lax.broadcasted_iota(jnp.int32, (blk, blk), 1)
    j0 = pl.multiple_of(j * blk, blk)
    s = _dot_nt(q, k_ref[0, pl.ds(j0, blk), :]) * scale + bias_ref[0, 0]
    s = jnp.where(row >= col, s, -jnp.inf)
    m0 = jnp.max(s, axis=1, keepdims=True)
    p = jnp.exp(s - m0)
    l0 = jnp.sum(p, axis=1, keepdims=True)
    acc0 = _dot(p.astype(BF16), v_ref[0, pl.ds(j0, blk), :])

    sub = lax.broadcasted_iota(jnp.int32, (LANES, blk), 0)

    def body(n, carry):
        m, l, acc = carry
        n0 = pl.multiple_of(n * blk, blk)
        d = jnp.minimum(j - n, n_bias - 1)
        onehot = jnp.where(sub == n, 1.0, 0.0).astype(BF16)
        s = _dot_nt(q, k_ref[0, pl.ds(n0, blk), :]) * scale + bias_ref[0, d]
        s = s + _dot(selneg, onehot)
        m_new = jnp.maximum(m, jnp.max(s, axis=1, keepdims=True))
        alpha = jnp.exp(m - m_new)
        p = jnp.exp(s - m_new)
        l = alpha * l + jnp.sum(p, axis=1, keepdims=True)
        acc = alpha * acc + _dot(p.astype(BF16), v_ref[0, pl.ds(n0, blk), :])
        return m_new, l, acc

    m, l, acc = lax.fori_loop(0, j, body, (m0, l0, acc0))
    o_ref[0] = (acc / l).astype(o_ref.dtype)


def moba_attention(qkv, tiles, batch, seq):
    d3 = qkv.shape[1]
    d = d3 // 3
    dh = d // N_HEADS
    nb = seq // MOBA_BLOCK
    n_bias = tiles.shape[1]
    qkv3 = qkv.reshape(batch, seq, d3)
    kern = functools.partial(_moba_kernel, nb=nb, n_bias=n_bias, scale=dh ** -0.5)
    out = pl.pallas_call(
        kern,
        grid=(batch, N_HEADS, nb),
        in_specs=[pl.BlockSpec((1, MOBA_BLOCK, dh), lambda b, h, j: (b, j, h)),
                  pl.BlockSpec((1, seq, dh), lambda b, h, j: (b, 0, N_HEADS + h)),
                  pl.BlockSpec((1, seq, dh), lambda b, h, j: (b, 0, 2 * N_HEADS + h)),
                  pl.BlockSpec((1, n_bias, MOBA_BLOCK, MOBA_BLOCK), lambda b, h, j: (h, 0, 0, 0))],
        out_specs=pl.BlockSpec((1, MOBA_BLOCK, dh), lambda b, h, j: (b, j, h)),
        out_shape=jax.ShapeDtypeStruct((batch, seq, d), BF16),
        scratch_shapes=[pltpu.VMEM((LANES, dh), F32)],
        compiler_params=_params(("parallel", "parallel", "arbitrary")),
        name="moba_attention",
    )(qkv3, qkv3, qkv3, tiles)
    return out.reshape(batch * seq, d)


def _dwconv_ln_kernel(prev_ref, cur_ref, w_ref, bdw_ref, g_ref, b_ref, o_ref, buf, *, halo):
    i = pl.program_id(1)
    ts = cur_ref.shape[1]
    prev = prev_ref[0]
    buf[0:halo, :] = jnp.where(i > 0, prev, jnp.zeros_like(prev))
    buf[halo:halo + ts, :] = cur_ref[0]
    acc = jnp.zeros(cur_ref.shape[1:], F32) + bdw_ref[...]
    for k in range(CONV_K):
        off = halo - (CONV_K - 1) + k
        acc = acc + buf[off:off + ts, :] * w_ref[k:k + 1, :]
    y = _layer_norm(acc, g_ref[...], b_ref[...])
    o_ref[0] = (y * jax.nn.sigmoid(y)).astype(o_ref.dtype)


def dwconv_ln_silu(h, w_dw, b_dw, g, b, batch, seq, ts=256):
    d = h.shape[1]
    ts = _pick(seq, ts)
    halo = 32
    assert halo >= CONV_K - 1 and ts % halo == 0
    r = ts // halo
    h3 = h.reshape(batch, seq, d)
    fix = lambda bi, i: (0, 0)
    kern = functools.partial(_dwconv_ln_kernel, halo=halo)
    out = pl.pallas_call(
        kern,
        grid=(batch, seq // ts),
        in_specs=[pl.BlockSpec((1, halo, d), lambda bi, i: (bi, jnp.maximum(i * r - 1, 0), 0)),
                  pl.BlockSpec((1, ts, d), lambda bi, i: (bi, i, 0)),
                  pl.BlockSpec((CONV_K, d), fix), pl.BlockSpec((1, d), fix),
                  pl.BlockSpec((1, d), fix), pl.BlockSpec((1, d), fix)],
        out_specs=pl.BlockSpec((1, ts, d), lambda bi, i: (bi, i, 0)),
        out_shape=jax.ShapeDtypeStruct((batch, seq, d), BF16),
        scratch_shapes=[pltpu.VMEM((halo + ts, d), F32)],
        compiler_params=_params(("parallel", "parallel")),
        name="dwconv_ln_silu",
    )(h3, h3, w_dw, b_dw.reshape(1, d), g.reshape(1, d), b.reshape(1, d))
    return out.reshape(batch * seq, d)


def _router_kernel(x_ref, wr_ref, rb_ref, idx_ref, wt_ref, *, n_exp):
    tm = x_ref.shape[0]
    per_g = n_exp // N_GROUPS
    lane = lax.broadcasted_iota(jnp.int32, (tm, LANES), 1)
    valid = lane < n_exp
    logits = jnp.dot(x_ref[...], wr_ref[...], preferred_element_type=F32,
                     precision=lax.Precision.HIGHEST)
    scores = jax.nn.sigmoid(logits)
    biased = jnp.where(valid, scores + rb_ref[...], -jnp.inf)
    grp = lane // per_g

    def first_argmax(v):
        mx = jnp.max(v, axis=1, keepdims=True)
        first = jnp.min(jnp.where(v == mx, lane, LANES), axis=1, keepdims=True)
        return mx, first

    gscore = jnp.full((tm, LANES), -jnp.inf, F32)
    for g in range(N_GROUPS):
        vg = jnp.where(grp == g, biased, -jnp.inf)
        m1, f1 = first_argmax(vg)
        m2 = jnp.max(jnp.where(lane == f1, -jnp.inf, vg), axis=1, keepdims=True)
        gscore = jnp.where(lane == g, m1 + m2, gscore)
    emask = jnp.zeros((tm, LANES), jnp.bool_)
    for _ in range(TOPK_GROUPS):
        _, fg = first_argmax(gscore)
        emask = emask | (grp == fg)
        gscore = jnp.where(lane == fg, -jnp.inf, gscore)
    cand = jnp.where(emask & valid, biased, -jnp.inf)
    idx = jnp.zeros((tm, LANES), jnp.int32)
    wts = jnp.zeros((tm, LANES), F32)
    for k in range(TOP_K):
        _, fe = first_argmax(cand)
        hit = lane == fe
        wk = jnp.sum(jnp.where(hit, scores, 0.0), axis=1, keepdims=True)
        idx = jnp.where(lane == k, fe, idx)
        wts = jnp.where(lane == k, wk, wts)
        cand = jnp.where(hit, -jnp.inf, cand)
    wsum = jnp.sum(wts, axis=1, keepdims=True)
    idx_ref[...] = idx
    wt_ref[...] = wts / wsum * ROUTED_SCALE


def router(x, w_router, r_bias, tm=256):
    n, d = x.shape
    n_exp = w_router.shape[1]
    assert n_exp <= LANES and n_exp % N_GROUPS == 0
    tm = _pick(n, tm)
    wr = jnp.pad(w_router, ((0, 0), (0, LANES - n_exp)))
    rb = jnp.pad(r_bias, (0, LANES - n_exp)).reshape(1, LANES)
    row = lambda i: (i, 0)
    fix = lambda i: (0, 0)
    idx, wts = pl.pallas_call(
        functools.partial(_router_kernel, n_exp=n_exp),
        grid=(n // tm,),
        in_specs=[pl.BlockSpec((tm, d), row), pl.BlockSpec((d, LANES), fix), pl.BlockSpec((1, LANES), fix)],
        out_specs=[pl.BlockSpec((tm, LANES), row), pl.BlockSpec((tm, LANES), row)],
        out_shape=[jax.ShapeDtypeStruct((n, LANES), jnp.int32), jax.ShapeDtypeStruct((n, LANES), F32)],
        compiler_params=_params(("parallel",)),
        name="router",
    )(x, wr, rb)
    return idx[:, :TOP_K], wts[:, :TOP_K]


def dispatch_tables(idx, wts, n_exp, tm):
    n_tok, k = idx.shape
    n_asg = n_tok * k
    n_tiles = n_asg // tm + n_exp
    e_flat = idx.reshape(-1)
    order = jnp.argsort(e_flat).astype(jnp.int32)
    e_sorted = e_flat[order]
    counts = jnp.zeros((n_exp,), jnp.int32).at[e_flat].add(1)
    starts = jnp.cumsum(counts) - counts
    tiles_e = (counts + tm - 1) // tm
    tile_end = jnp.cumsum(tiles_e)
    tile_start = tile_end - tiles_e
    n_used = tile_end[-1]
    pos = tile_start[e_sorted] * tm + (jnp.arange(n_asg, dtype=jnp.int32) - starts[e_sorted])
    n_rows = n_tiles * tm
    tok = jnp.zeros((n_rows,), jnp.int32).at[pos].set(order // k)
    dst = (n_asg + jnp.arange(n_rows, dtype=jnp.int32) % tm).at[pos].set(order)
    w_row = jnp.zeros((n_rows,), F32).at[pos].set(wts.reshape(-1)[order])
    tile_ids = jnp.minimum(jnp.arange(n_tiles, dtype=jnp.int32), n_used - 1)
    tile_exp = jnp.searchsorted(tile_end, tile_ids, side="right").astype(jnp.int32)
    tile_exp = jnp.minimum(tile_exp, n_exp - 1)
    return (tile_exp, n_used.reshape(1).astype(jnp.int32), tok.reshape(n_tiles, 1, tm),
            dst.reshape(n_tiles, 1, tm), w_row.reshape(n_tiles, tm, 1))


def _experts_kernel(te_ref, nused_ref, tok_ref, dst_ref, w_ref, x_hbm, wg_ref, wu_ref, wd_ref,
                    y_hbm, xs, ys, gsem, ssem):
    i = pl.program_id(0)
    tm = xs.shape[0]

    @pl.when(i < nused_ref[0])
    def _():
        def gather(r, c):
            pltpu.make_async_copy(x_hbm.at[pl.ds(tok_ref[0, 0, r], 1)], xs.at[pl.ds(r, 1)], gsem).start()
            return c
        lax.fori_loop(0, tm, gather, 0)
        pltpu.make_async_copy(x_hbm.at[pl.ds(0, tm)], xs, gsem).wait()

        x = xs[...].astype(BF16)
        g = _dot(x, wg_ref[0].astype(BF16))
        u = _dot(x, wu_ref[0].astype(BF16))
        h = (g * jax.nn.sigmoid(g) * u).astype(BF16)
        ys[...] = _dot(h, wd_ref[0].astype(BF16)) * w_ref[0]

        @pl.when(i == 0)
        def _():
            fill = pltpu.make_async_copy(ys, y_hbm.at[pl.ds(y_hbm.shape[0] - tm, tm)], ssem)
            fill.start()
            fill.wait()

        def scatter(r, c):
            pltpu.make_async_copy(ys.at[pl.ds(r, 1)], y_hbm.at[pl.ds(dst_ref[0, 0, r], 1)], ssem).start()
            return c
        lax.fori_loop(0, tm, scatter, 0)
        pltpu.make_async_copy(ys, y_hbm.at[pl.ds(0, tm)], ssem).wait()


def routed_experts(x, idx, wts, w_gate, w_up, w_down, tm=256):
    n_tok, d = x.shape
    n_exp, _, f = w_gate.shape
    n_asg = n_tok * idx.shape[1]
    tm = _pick(n_asg, tm)
    tile_exp, n_used, tok, dst, w_row = dispatch_tables(idx, wts, n_exp, tm)
    n_tiles = tile_exp.shape[0]
    smem_blk = pl.BlockSpec((1, 1, tm), lambda i, te, nu: (i, 0, 0), memory_space=pltpu.SMEM)
    grid_spec = pltpu.PrefetchScalarGridSpec(
        num_scalar_prefetch=2,
        grid=(n_tiles,),
        in_specs=[smem_blk, smem_blk,
                  pl.BlockSpec((1, tm, 1), lambda i, te, nu: (i, 0, 0)),
                  pl.BlockSpec(memory_space=pl.ANY),
                  pl.BlockSpec((1, d, f), lambda i, te, nu: (te[i], 0, 0)),
                  pl.BlockSpec((1, d, f), lambda i, te, nu: (te[i], 0, 0)),
                  pl.BlockSpec((1, f, d), lambda i, te, nu: (te[i], 0, 0))],
        out_specs=pl.BlockSpec(memory_space=pl.ANY),
        scratch_shapes=[pltpu.VMEM((tm, d), F32), pltpu.VMEM((tm, d), F32),
                        pltpu.SemaphoreType.DMA, pltpu.SemaphoreType.DMA],
    )
    y = pl.pallas_call(
        _experts_kernel,
        grid_spec=grid_spec,
        out_shape=jax.ShapeDtypeStruct((n_asg + tm, d), F32),
        compiler_params=_params(("arbitrary",)),
        name="routed_experts",
    )(tile_exp, n_used, tok, dst, w_row, x, w_gate, w_up, w_down)
    return y


def _combine_kernel(y_ref, xb_ref, sg_ref, su_ref, sd_ref, res_ref, g_ref, b_ref, o_ref, ob_ref, *, k):
    d = res_ref.shape[1]
    routed = y_ref[:, 0:d]
    for s in range(1, k):
        routed = routed + y_ref[:, s * d:(s + 1) * d]
    xb = xb_ref[...]
    hg = _dot(xb, sg_ref[...])
    hu = _dot(xb, su_ref[...])
    shared = _dot((hg * jax.nn.sigmoid(hg) * hu).astype(BF16), sd_ref[...])
    z = _layer_norm(DN_ALPHA * res_ref[...] + (routed + shared), g_ref[...], b_ref[...])
    o_ref[...] = z
    ob_ref[...] = z.astype(BF16)


def combine_shared_ln(y, xb, s_gate, s_up, s_down, res, g, b, tm=128):
    n, d = res.shape
    f = s_gate.shape[1]
    k = TOP_K
    tm = _pick(n, tm)
    y2 = y.reshape(y.shape[0] // k, k * d)
    row = lambda i: (i, 0)
    fix = lambda i: (0, 0)
    return pl.pallas_call(
        functools.partial(_combine_kernel, k=k),
        grid=(n // tm,),
        in_specs=[pl.BlockSpec((tm, k * d), row), pl.BlockSpec((tm, d), row),
                  pl.BlockSpec((d, f), fix), pl.BlockSpec((d, f), fix), pl.BlockSpec((f, d), fix),
                  pl.BlockSpec((tm, d), row), pl.BlockSpec((1, d), fix), pl.BlockSpec((1, d), fix)],
        out_specs=[pl.BlockSpec((tm, d), row), pl.BlockSpec((tm, d), row)],
        out_shape=[jax.ShapeDtypeStruct((n, d), F32), jax.ShapeDtypeStruct((n, d), BF16)],
        compiler_params=_params(("parallel",)),
        name="combine_shared_ln",
    )(y2, xb, s_gate, s_up, s_down, res, g.reshape(1, d), b.reshape(1, d))


def kernel(x, rel_bias, attn_w_qkv, attn_w_o, conv_w_in, conv_b_in, conv_w_dw, conv_b_dw,
           conv_ln_g, conv_ln_b, conv_w_out, conv_b_out, ln_mix_g, ln_mix_b, ln_ffn_g, ln_ffn_b,
           moe_w_router, moe_router_bias, moe_w_gate, moe_w_up, moe_w_down,
           shared_w_gate, shared_w_up, shared_w_down):
    batch, seq, d = x.shape
    n = batch * seq
    depth = ln_mix_g.shape[0]
    n_exp = moe_w_router.shape[2]
    nb = seq // MOBA_BLOCK
    assert seq % MOBA_BLOCK == 0 and d % N_HEADS == 0 and depth == DEPTH

    tiles = bias_tiles(rel_bias, N_HEADS, _num_bias_tiles(nb))
    xf = x.reshape(n, d)
    xb = xf.astype(BF16)
    zero_bias = jnp.zeros((d,), F32)
    for i in range(depth):
        m = i // 2
        if i % 2 == 0:
            qkv = matmul(xb, attn_w_qkv[m].astype(BF16), BF16)
            a = moba_attention(qkv, tiles, batch, seq)
            xf, xb = proj_res_ln(a, attn_w_o[m].astype(BF16), zero_bias, xf, ln_mix_g[i], ln_mix_b[i])
        else:
            h = glu_proj(xb, conv_w_in[m].astype(BF16), conv_b_in[m])
            a = dwconv_ln_silu(h, conv_w_dw[m], conv_b_dw[m], conv_ln_g[m], conv_ln_b[m], batch, seq)
            xf, xb = proj_res_ln(a, conv_w_out[m].astype(BF16), conv_b_out[m], xf, ln_mix_g[i], ln_mix_b[i])
        idx, wts = router(xf, moe_w_router[i], moe_router_bias[i])
        y = routed_experts(xf, idx, wts, moe_w_gate[i], moe_w_up[i], moe_w_down[i])
        xf, xb = combine_shared_ln(y, xb, shared_w_gate[i].astype(BF16), shared_w_up[i].astype(BF16),
                                   shared_w_down[i].astype(BF16), xf, ln_ffn_g[i], ln_ffn_b[i])
    return xf.reshape(batch, seq, d)
```

```python
import functools
import math

import jax
import jax.numpy as jnp
from jax import lax
from jax.experimental import pallas as pl
from jax.experimental.pallas import tpu as pltpu

N_HEADS = 16
MOBA_BLOCK = 256
MOBA_TOPK = 3
REL_BUCKETS = 32
REL_MAX_DIST = 2048
CONV_K = 31
TOP_K = 8
N_GROUPS = 8
TOPK_GROUPS = 4
ROUTED_SCALE = 2.5
DEPTH = 4
DN_ALPHA = (2 * DEPTH) ** 0.25
LN_EPS = 1e-5

LANES = 128
SUBLANES = 8
VMEM_LIMIT_BYTES = 56 * 1024 * 1024

NEG_BIG = -1e30
KV_STEP = 2

F32 = jnp.float32
BF16 = jnp.bfloat16


def _params(sem):
    return pltpu.CompilerParams(dimension_semantics=sem, vmem_limit_bytes=VMEM_LIMIT_BYTES)


def _pick(n, pref):
    t = min(pref, n)
    while n % t:
        t //= 2
    return t


def _layer_norm(z, g, b):
    mu = jnp.mean(z, axis=-1, keepdims=True)
    zc = z - mu
    var = jnp.mean(zc * zc, axis=-1, keepdims=True)
    return zc * lax.rsqrt(var + LN_EPS) * g + b


def _dot(a, b):
    return jnp.dot(a, b, preferred_element_type=F32)


def _dot_nt(a, b):
    return lax.dot_general(a, b, (((1,), (1,)), ((), ())), preferred_element_type=F32)


def _mm_kernel(x_ref, w_ref, o_ref):
    o_ref[...] = _dot(x_ref[...], w_ref[...]).astype(o_ref.dtype)


def matmul(x, w, out_dtype, tm=1024, tn=512):
    n, k = x.shape
    m = w.shape[1]
    tm, tn = _pick(n, tm), _pick(m, tn)
    return pl.pallas_call(
        _mm_kernel,
        grid=(n // tm, m // tn),
        in_specs=[pl.BlockSpec((tm, k), lambda i, j: (i, 0)),
                  pl.BlockSpec((k, tn), lambda i, j: (0, j))],
        out_specs=pl.BlockSpec((tm, tn), lambda i, j: (i, j)),
        out_shape=jax.ShapeDtypeStruct((n, m), out_dtype),
        compiler_params=_params(("parallel", "arbitrary")),
        name="matmul",
    )(x, w)


def _glu_kernel(x_ref, wa_ref, wg_ref, ba_ref, bg_ref, o_ref):
    x = x_ref[...]
    a = _dot(x, wa_ref[...]) + ba_ref[...]
    g = _dot(x, wg_ref[...]) + bg_ref[...]
    o_ref[...] = a * jax.nn.sigmoid(g)


def glu_proj(x, w, b, tm=1024, tn=256):
    n, k = x.shape
    d = w.shape[1] // 2
    tm, tn = _pick(n, tm), _pick(d, tn)
    nj = d // tn
    b2 = b.reshape(1, 2 * d)
    return pl.pallas_call(
        _glu_kernel,
        grid=(n // tm, nj),
        in_specs=[pl.BlockSpec((tm, k), lambda i, j: (i, 0)),
                  pl.BlockSpec((k, tn), lambda i, j: (0, j)),
                  pl.BlockSpec((k, tn), lambda i, j: (0, j + nj)),
                  pl.BlockSpec((1, tn), lambda i, j: (0, j)),
                  pl.BlockSpec((1, tn), lambda i, j: (0, j + nj))],
        out_specs=pl.BlockSpec((tm, tn), lambda i, j: (i, j)),
        out_shape=jax.ShapeDtypeStruct((n, d), F32),
        compiler_params=_params(("parallel", "arbitrary")),
        name="glu_proj",
    )(x, w, w, b2, b2)


def _proj_res_ln_kernel(a_ref, w_ref, bias_ref, res_ref, g_ref, b_ref, o_ref, ob_ref):
    y = _dot(a_ref[...], w_ref[...]) + bias_ref[...]
    z = _layer_norm(DN_ALPHA * res_ref[...] + y, g_ref[...], b_ref[...])
    o_ref[...] = z
    ob_ref[...] = z.astype(BF16)


def proj_res_ln(a, w, bias, res, g, b, tm=256):
    n, k = a.shape
    d = w.shape[1]
    tm = _pick(n, tm)
    row = lambda i: (i, 0)
    fix = lambda i: (0, 0)
    return pl.pallas_call(
        _proj_res_ln_kernel,
        grid=(n // tm,),
        in_specs=[pl.BlockSpec((tm, k), row), pl.BlockSpec((k, d), fix),
                  pl.BlockSpec((1, d), fix), pl.BlockSpec((tm, d), row),
                  pl.BlockSpec((1, d), fix), pl.BlockSpec((1, d), fix)],
        out_specs=[pl.BlockSpec((tm, d), row), pl.BlockSpec((tm, d), row)],
        out_shape=[jax.ShapeDtypeStruct((n, d), F32), jax.ShapeDtypeStruct((n, d), BF16)],
        compiler_params=_params(("parallel",)),
        name="proj_res_ln",
    )(a, w, bias.reshape(1, d), res, g.reshape(1, d), b.reshape(1, d))


def _t5_bucket(rel):
    n = jnp.maximum(rel, 0)
    max_exact = REL_BUCKETS // 2
    nf = jnp.maximum(n, 1).astype(F32)
    large = max_exact + (jnp.log(nf / max_exact) / math.log(REL_MAX_DIST / max_exact)
                         * (REL_BUCKETS - max_exact)).astype(jnp.int32)
    large = jnp.minimum(large, REL_BUCKETS - 1)
    return jnp.where(n < max_exact, n, large)


def _bias_tile_kernel(rb_ref, o_ref):
    h = pl.program_id(0)
    d = pl.program_id(1)
    blk = o_ref.shape[-1]
    row = lax.broadcasted_iota(jnp.int32, (blk, blk), 0)
    col = lax.broadcasted_iota(jnp.int32, (blk, blk), 1)
    bucket = _t5_bucket(d * blk + row - col)
    acc = jnp.zeros((blk, blk), F32)
    for k in range(REL_BUCKETS):
        acc = jnp.where(bucket == k, rb_ref[k, h], acc)
    o_ref[0, 0] = acc


def _num_bias_tiles(nb):
    last_start = (REL_MAX_DIST / (REL_BUCKETS // 2)) ** ((REL_BUCKETS // 2 - 1) / (REL_BUCKETS // 2)) \
        * (REL_BUCKETS // 2)
    d = 1
    while (d - 1) * MOBA_BLOCK + 1 < 1.05 * last_start + 1:
        d += 1
    return min(nb, d + 1)


def bias_tiles(rel_bias, n_heads, n_tiles):
    return pl.pallas_call(
        _bias_tile_kernel,
        grid=(n_heads, n_tiles),
        in_specs=[pl.BlockSpec(memory_space=pltpu.SMEM)],
        out_specs=pl.BlockSpec((1, 1, MOBA_BLOCK, MOBA_BLOCK), lambda h, d: (h, d, 0, 0)),
        out_shape=jax.ShapeDtypeStruct((n_heads, n_tiles, MOBA_BLOCK, MOBA_BLOCK), F32),
        compiler_params=_params(("parallel", "parallel")),
        name="bias_tiles",
    )(rel_bias)


def _moba_kernel(q_ref, k_ref, v_ref, bias_ref, o_ref, kmean_sc, *, nb, n_bias, scale):
    j = pl.program_id(2)
    blk = MOBA_BLOCK

    @pl.when(j == 0)
    def _():
        kmean_sc[...] = jnp.zeros_like(kmean_sc)
        for n in range(nb):
            kn = k_ref[0, n * blk:(n + 1) * blk, :].astype(F32)
            kmean_sc[n:n + 1, :] = jnp.mean(kn, axis=0, keepdims=True)

    q = q_ref[0]
    lane = lax.broadcasted_iota(jnp.int32, (blk, LANES), 1)
    lane_f = lane.astype(F32)

    gate = _dot_nt(q, kmean_sc[...].astype(BF16))
    gate = jnp.where(lane < j, gate, -jnp.inf)
    sel = jnp.zeros((blk, LANES), F32)
    for _ in range(MOBA_TOPK):
        mx = jnp.max(gate, axis=1, keepdims=True)
        first = jnp.min(jnp.where(gate == mx, lane_f, float(LANES)), axis=1, keepdims=True)
        hit = lane_f == first
        sel = jnp.where(hit, jnp.where(mx > -jnp.inf, 1.0, sel), sel)
        gate = jnp.where(hit, -jnp.inf, gate)
    selneg = jnp.where(sel > 0.0, 0.0, NEG_BIG).astype(BF16)

    row = lax.broadcasted_iota(jnp.int32, (blk, blk), 0)
    col = lax.broadcasted_iota(jnp.int32, (blk, blk), 1)
    j0 = pl.multiple_of(j * blk, blk)
    s = _dot_nt(q, k_ref[0, pl.ds(j0, blk), :]) * scale + bias_ref[0, 0]
    s = jnp.where(row >= col, s, -jnp.inf)
    m0 = jnp.max(s, axis=1, keepdims=True)
    p = jnp.exp(s - m0)
    l0 = jnp.sum(p, axis=1, keepdims=True)
    acc0 = _dot(p.astype(BF16), v_ref[0, pl.ds(j0, blk), :])

    cw = KV_STEP * blk
    cand = lax.broadcasted_iota(jnp.int32, (LANES, cw), 0)
    sub_blk = lax.broadcasted_iota(jnp.int32, (LANES, cw), 1) // blk

    def body(c, carry):
        m, l, acc = carry
        c0 = pl.multiple_of(c * cw, cw)
        onehot = jnp.where(cand == c * KV_STEP + sub_blk, 1.0, 0.0).astype(BF16)
        bias = jnp.concatenate(
            [bias_ref[0, jnp.clip(j - (c * KV_STEP + u), 0, n_bias - 1)] for u in range(KV_STEP)], axis=1)
        s = _dot_nt(q, k_ref[0, pl.ds(c0, cw), :]) * scale + bias
        s = s + _dot(selneg, onehot)
        m_new = jnp.maximum(m, jnp.max(s, axis=1, keepdims=True))
        alpha = jnp.exp(m - m_new)
        p = jnp.exp(s - m_new)
        l = alpha * l + jnp.sum(p, axis=1, keepdims=True)
        acc = alpha * acc + _dot(p.astype(BF16), v_ref[0, pl.ds(c0, cw), :])
        return m_new, l, acc

    m, l, acc = lax.fori_loop(0, (j + KV_STEP - 1) // KV_STEP, body, (m0, l0, acc0))
    o_ref[0] = (acc / l).astype(o_ref.dtype)


def moba_attention(qkv, tiles, batch, seq):
    d3 = qkv.shape[1]
    d = d3 // 3
    dh = d // N_HEADS
    nb = seq // MOBA_BLOCK
    assert nb % KV_STEP == 0
    n_bias = tiles.shape[1]
    qkv3 = qkv.reshape(batch, seq, d3)
    kern = functools.partial(_moba_kernel, nb=nb, n_bias=n_bias, scale=dh ** -0.5)
    out = pl.pallas_call(
        kern,
        grid=(batch, N_HEADS, nb),
        in_specs=[pl.BlockSpec((1, MOBA_BLOCK, dh), lambda b, h, j: (b, j, h)),
                  pl.BlockSpec((1, seq, dh), lambda b, h, j: (b, 0, N_HEADS + h)),
                  pl.BlockSpec((1, seq, dh), lambda b, h, j: (b, 0, 2 * N_HEADS + h)),
                  pl.BlockSpec((1, n_bias, MOBA_BLOCK, MOBA_BLOCK), lambda b, h, j: (h, 0, 0, 0))],
        out_specs=pl.BlockSpec((1, MOBA_BLOCK, dh), lambda b, h, j: (b, j, h)),
        out_shape=jax.ShapeDtypeStruct((batch, seq, d), BF16),
        scratch_shapes=[pltpu.VMEM((LANES, dh), F32)],
        compiler_params=_params(("parallel", "parallel", "arbitrary")),
        name="moba_attention",
    )(qkv3, qkv3, qkv3, tiles)
    return out.reshape(batch * seq, d)


def _dwconv_ln_kernel(prev_ref, cur_ref, w_ref, bdw_ref, g_ref, b_ref, o_ref, buf, *, halo):
    i = pl.program_id(1)
    ts = cur_ref.shape[1]
    prev = prev_ref[0]
    buf[0:halo, :] = jnp.where(i > 0, prev, jnp.zeros_like(prev))
    buf[halo:halo + ts, :] = cur_ref[0]
    acc = jnp.zeros(cur_ref.shape[1:], F32) + bdw_ref[...]
    for k in range(CONV_K):
        off = halo - (CONV_K - 1) + k
        acc = acc + buf[off:off + ts, :] * w_ref[k:k + 1, :]
    y = _layer_norm(acc, g_ref[...], b_ref[...])
    o_ref[0] = (y * jax.nn.sigmoid(y)).astype(o_ref.dtype)


def dwconv_ln_silu(h, w_dw, b_dw, g, b, batch, seq, ts=256):
    d = h.shape[1]
    ts = _pick(seq, ts)
    halo = 32
    assert halo >= CONV_K - 1 and ts % halo == 0
    r = ts // halo
    h3 = h.reshape(batch, seq, d)
    fix = lambda bi, i: (0, 0)
    kern = functools.partial(_dwconv_ln_kernel, halo=halo)
    out = pl.pallas_call(
        kern,
        grid=(batch, seq // ts),
        in_specs=[pl.BlockSpec((1, halo, d), lambda bi, i: (bi, jnp.maximum(i * r - 1, 0), 0)),
                  pl.BlockSpec((1, ts, d), lambda bi, i: (bi, i, 0)),
                  pl.BlockSpec((CONV_K, d), fix), pl.BlockSpec((1, d), fix),
                  pl.BlockSpec((1, d), fix), pl.BlockSpec((1, d), fix)],
        out_specs=pl.BlockSpec((1, ts, d), lambda bi, i: (bi, i, 0)),
        out_shape=jax.ShapeDtypeStruct((batch, seq, d), BF16),
        scratch_shapes=[pltpu.VMEM((halo + ts, d), F32)],
        compiler_params=_params(("parallel", "parallel")),
        name="dwconv_ln_silu",
    )(h3, h3, w_dw, b_dw.reshape(1, d), g.reshape(1, d), b.reshape(1, d))
    return out.reshape(batch * seq, d)


def _router_kernel(x_ref, wr_ref, rb_ref, idx_ref, wt_ref, rank_ref, cnt_ref, carry, *, n_exp):
    step = pl.program_id(0)
    tm = x_ref.shape[0]
    per_g = n_exp // N_GROUPS
    lane = lax.broadcasted_iota(jnp.int32, (tm, LANES), 1)
    valid = lane < n_exp
    logits = jnp.dot(x_ref[...], wr_ref[...], preferred_element_type=F32,
                     precision=lax.Precision.HIGHEST)
    scores = jax.nn.sigmoid(logits)
    biased = jnp.where(valid, scores + rb_ref[...], -jnp.inf)
    grp = lane // per_g

    def first_argmax(v):
        mx = jnp.max(v, axis=1, keepdims=True)
        first = jnp.min(jnp.where(v == mx, lane, LANES), axis=1, keepdims=True)
        return mx, first

    gscore = jnp.full((tm, LANES), -jnp.inf, F32)
    for g in range(N_GROUPS):
        vg = jnp.where(grp == g, biased, -jnp.inf)
        m1, f1 = first_argmax(vg)
        m2 = jnp.max(jnp.where(lane == f1, -jnp.inf, vg), axis=1, keepdims=True)
        gscore = jnp.where(lane == g, m1 + m2, gscore)
    emask = jnp.zeros((tm, LANES), jnp.bool_)
    for _ in range(TOPK_GROUPS):
        _, fg = first_argmax(gscore)
        emask = emask | (grp == fg)
        gscore = jnp.where(lane == fg, -jnp.inf, gscore)
    cand = jnp.where(emask & valid, biased, -jnp.inf)
    idx = jnp.zeros((tm, LANES), jnp.int32)
    wts = jnp.zeros((tm, LANES), F32)
    chosen = jnp.zeros((tm, LANES), F32)
    picks = []
    for k in range(TOP_K):
        _, fe = first_argmax(cand)
        hit = lane == fe
        wk = jnp.sum(jnp.where(hit, scores, 0.0), axis=1, keepdims=True)
        idx = jnp.where(lane == k, fe, idx)
        wts = jnp.where(lane == k, wk, wts)
        chosen = jnp.where(hit, 1.0, chosen)
        cand = jnp.where(hit, -jnp.inf, cand)
        picks.append(fe)
    wsum = jnp.sum(wts, axis=1, keepdims=True)
    idx_ref[...] = idx
    wt_ref[...] = wts / wsum * ROUTED_SCALE

    @pl.when(step == 0)
    def _():
        carry[...] = jnp.zeros_like(carry)

    r_i = lax.broadcasted_iota(jnp.int32, (tm, tm), 0)
    c_i = lax.broadcasted_iota(jnp.int32, (tm, tm), 1)
    earlier = jnp.where(r_i > c_i, 1.0, 0.0).astype(BF16)
    rank = _dot(earlier, chosen.astype(BF16)) + carry[...]
    rank_sel = jnp.zeros((tm, LANES), F32)
    for k in range(TOP_K):
        rk = jnp.sum(jnp.where(lane == picks[k], rank, 0.0), axis=1, keepdims=True)
        rank_sel = jnp.where(lane == k, rk, rank_sel)
    rank_ref[...] = rank_sel.astype(jnp.int32)
    total = carry[...] + jnp.sum(chosen, axis=0, keepdims=True)
    carry[...] = total
    cnt_ref[...] = total.astype(jnp.int32)


def router(x, w_router, r_bias, tm=256):
    n, d = x.shape
    n_exp = w_router.shape[1]
    assert n_exp <= LANES and n_exp % N_GROUPS == 0
    tm = _pick(n, tm)
    wr = jnp.pad(w_router, ((0, 0), (0, LANES - n_exp)))
    rb = jnp.pad(r_bias, (0, LANES - n_exp)).reshape(1, LANES)
    row = lambda i: (i, 0)
    fix = lambda i: (0, 0)
    return pl.pallas_call(
        functools.partial(_router_kernel, n_exp=n_exp),
        grid=(n // tm,),
        in_specs=[pl.BlockSpec((tm, d), row), pl.BlockSpec((d, LANES), fix), pl.BlockSpec((1, LANES), fix)],
        out_specs=[pl.BlockSpec((tm, LANES), row), pl.BlockSpec((tm, LANES), row),
                   pl.BlockSpec((tm, LANES), row), pl.BlockSpec((1, LANES), fix)],
        out_shape=[jax.ShapeDtypeStruct((n, LANES), jnp.int32), jax.ShapeDtypeStruct((n, LANES), F32),
                   jax.ShapeDtypeStruct((n, LANES), jnp.int32), jax.ShapeDtypeStruct((1, LANES), jnp.int32)],
        scratch_shapes=[pltpu.VMEM((1, LANES), F32)],
        compiler_params=_params(("arbitrary",)),
        name="router",
    )(x, wr, rb)


def expert_plan(idx, rank, counts, n_exp, tm):
    n_tok = idx.shape[0]
    n_asg = n_tok * TOP_K
    n_tiles = n_asg // tm
    n_items = n_tiles + n_exp - 1
    counts = counts[0, :n_exp]
    ends = jnp.cumsum(counts)
    starts = ends - counts
    e_ids = jnp.arange(n_exp, dtype=jnp.int32)
    idx_k = idx[:, :TOP_K]
    pos = rank[:, :TOP_K] + jnp.sum(jnp.where(idx_k[..., None] == e_ids, starts, 0), axis=-1)

    first_blk = starts // tm
    n_items_e = jnp.where(counts > 0, (ends - 1) // tm - first_blk + 1, 0)
    item_end = jnp.cumsum(n_items_e)
    item_start = item_end - n_items_e
    n_live = item_end[-1]
    it = jnp.minimum(jnp.arange(n_items, dtype=jnp.int32), n_live - 1)
    onehot = (jnp.sum(item_end[None, :] <= it[:, None], axis=1)[:, None] == e_ids).astype(jnp.int32)
    pick = lambda v: jnp.sum(onehot * v[None, :], axis=1)
    item_exp = pick(e_ids)
    item_blk = jnp.clip(pick(first_blk) + it - pick(item_start), 0, n_tiles - 1)
    lo = jnp.clip(pick(starts) - item_blk * tm, 0, tm)
    hi = jnp.clip(pick(ends) - item_blk * tm, 0, tm)
    one = jnp.ones((1,), jnp.int32)
    first = jnp.concatenate([one, (item_blk[1:] != item_blk[:-1]).astype(jnp.int32)])
    new_exp = jnp.concatenate([one, (item_exp[1:] != item_exp[:-1]).astype(jnp.int32)])
    tables = tuple(t.astype(jnp.int32)
                   for t in (item_blk, item_exp, lo, hi, first, new_exp, n_live.reshape(1)))
    return pos.astype(jnp.int32), tables


def _dispatch_kernel(pos_ref, x_ref, xs_hbm, sem):
    tt = x_ref.shape[0]

    def token(t, c):
        for k in range(TOP_K):
            pltpu.make_async_copy(x_ref.at[pl.ds(t, 1)], xs_hbm.at[pl.ds(pos_ref[0, 0, t * TOP_K + k], 1)],
                                  sem).start()
        return c
    lax.fori_loop(0, tt, token, 0)
    for _ in range(TOP_K):
        pltpu.make_async_copy(x_ref, xs_hbm.at[pl.ds(0, tt)], sem).wait()


def dispatch(x, pos, tt=256):
    n, d = x.shape
    tt = _pick(n, tt)
    pos3 = pos.reshape(n // tt, 1, tt * TOP_K)
    return pl.pallas_call(
        _dispatch_kernel,
        grid=(n // tt,),
        in_specs=[pl.BlockSpec((1, 1, tt * TOP_K), lambda i: (i, 0, 0), memory_space=pltpu.SMEM),
                  pl.BlockSpec((tt, d), lambda i: (i, 0))],
        out_specs=pl.BlockSpec(memory_space=pl.ANY),
        out_shape=jax.ShapeDtypeStruct((n * TOP_K, d), x.dtype),
        scratch_shapes=[pltpu.SemaphoreType.DMA],
        compiler_params=_params(("arbitrary",)),
        name="dispatch",
    )(pos3, x)


def _experts_kernel(blk_ref, exp_ref, lo_ref, hi_ref, first_ref, newexp_ref, nlive_ref,
                    xs_ref, wg_ref, wu_ref, wd_ref, y_ref, wgb, wub, wdb):
    i = pl.program_id(0)
    tm = xs_ref.shape[0]

    @pl.when(i < nlive_ref[0])
    def _():
        @pl.when(newexp_ref[i] == 1)
        def _():
            wgb[...] = wg_ref[...].astype(BF16)
            wub[...] = wu_ref[...].astype(BF16)
            wdb[...] = wd_ref[...].astype(BF16)

        x = xs_ref[...].astype(BF16)
        g = _dot(x, wgb[...])
        u = _dot(x, wub[...])
        h = (g * jax.nn.sigmoid(g) * u).astype(BF16)
        y = _dot(h, wdb[...])
        row = lax.broadcasted_iota(jnp.int32, (tm, 1), 0)
        y = jnp.where((row >= lo_ref[i]) & (row < hi_ref[i]), y, 0.0)

        @pl.when(first_ref[i] == 1)
        def _():
            y_ref[...] = y

        @pl.when(first_ref[i] == 0)
        def _():
            y_ref[...] += y


def routed_experts(xs, tables, w_gate, w_up, w_down, layer, tm):
    n_asg, d = xs.shape
    n_exp, f = w_gate.shape[1], w_gate.shape[3]
    n_items = n_asg // tm + n_exp - 1
    w_in = pl.BlockSpec((None, None, d, f), lambda i, blk, ex, *_: (layer, ex[i], 0, 0))
    w_out = pl.BlockSpec((None, None, f, d), lambda i, blk, ex, *_: (layer, ex[i], 0, 0))
    rows = pl.BlockSpec((tm, d), lambda i, blk, *_: (blk[i], 0))
    grid_spec = pltpu.PrefetchScalarGridSpec(
        num_scalar_prefetch=7,
        grid=(n_items,),
        in_specs=[rows, w_in, w_in, w_out],
        out_specs=rows,
        scratch_shapes=[pltpu.VMEM((d, f), BF16), pltpu.VMEM((d, f), BF16), pltpu.VMEM((f, d), BF16)],
    )
    return pl.pallas_call(
        _experts_kernel,
        grid_spec=grid_spec,
        out_shape=jax.ShapeDtypeStruct((n_asg, d), F32),
        compiler_params=_params(("arbitrary",)),
        name="routed_experts",
    )(*tables, xs, w_gate, w_up, w_down)


def _combine_kernel(pos_ref, y_hbm, wt_ref, xb_ref, sg_ref, su_ref, sd_ref, res_ref, g_ref, b_ref,
                    o_ref, ob_ref, ybuf, sem):
    tc = res_ref.shape[0]

    def token(t, c):
        for k in range(TOP_K):
            pltpu.make_async_copy(y_hbm.at[pl.ds(pos_ref[0, 0, t * TOP_K + k], 1)],
                                  ybuf.at[k, pl.ds(t, 1)], sem).start()
        return c
    lax.fori_loop(0, tc, token, 0)

    xb = xb_ref[...]
    hg = _dot(xb, sg_ref[...])
    hu = _dot(xb, su_ref[...])
    shared = _dot((hg * jax.nn.sigmoid(hg) * hu).astype(BF16), sd_ref[...])
    base = DN_ALPHA * res_ref[...] + shared

    for k in range(TOP_K):
        pltpu.make_async_copy(y_hbm.at[pl.ds(0, tc)], ybuf.at[k], sem).wait()
    wts = wt_ref[...]
    routed = ybuf[0] * wts[:, 0:1]
    for k in range(1, TOP_K):
        routed = routed + ybuf[k] * wts[:, k:k + 1]
    z = _layer_norm(base + routed, g_ref[...], b_ref[...])
    o_ref[...] = z
    ob_ref[...] = z.astype(BF16)


def combine_shared_ln(y, pos, wts, xb, s_gate, s_up, s_down, res, g, b, tc=128):
    n, d = res.shape
    f = s_gate.shape[1]
    tc = _pick(n, tc)
    pos3 = pos.reshape(n // tc, 1, tc * TOP_K)
    row = lambda i: (i, 0)
    fix = lambda i: (0, 0)
    return pl.pallas_call(
        _combine_kernel,
        grid=(n // tc,),
        in_specs=[pl.BlockSpec((1, 1, tc * TOP_K), lambda i: (i, 0, 0), memory_space=pltpu.SMEM),
                  pl.BlockSpec(memory_space=pl.ANY),
                  pl.BlockSpec((tc, LANES), row), pl.BlockSpec((tc, d), row),
                  pl.BlockSpec((d, f), fix), pl.BlockSpec((d, f), fix), pl.BlockSpec((f, d), fix),
                  pl.BlockSpec((tc, d), row), pl.BlockSpec((1, d), fix), pl.BlockSpec((1, d), fix)],
        out_specs=[pl.BlockSpec((tc, d), row), pl.BlockSpec((tc, d), row)],
        out_shape=[jax.ShapeDtypeStruct((n, d), F32), jax.ShapeDtypeStruct((n, d), BF16)],
        scratch_shapes=[pltpu.VMEM((TOP_K, tc, d), F32), pltpu.SemaphoreType.DMA],
        compiler_params=_params(("arbitrary",)),
        name="combine_shared_ln",
    )(pos3, y, wts, xb, s_gate, s_up, s_down, res, g.reshape(1, d), b.reshape(1, d))


def moe_ffn_ln(xf, xb, layer, w_router, r_bias, w_gate, w_up, w_down, s_gate, s_up, s_down, g, b, tm=256):
    n_exp = w_router.shape[1]
    tm = _pick(xf.shape[0] * TOP_K, tm)
    idx, wts, rank, counts = router(xf, w_router, r_bias)
    pos, tables = expert_plan(idx, rank, counts, n_exp, tm)
    xs = dispatch(xf, pos)
    y = routed_experts(xs, tables, w_gate, w_up, w_down, layer, tm)
    return combine_shared_ln(y, pos, wts, xb, s_gate.astype(BF16), s_up.astype(BF16), s_down.astype(BF16),
                             xf, g, b)


def kernel(x, rel_bias, attn_w_qkv, attn_w_o, conv_w_in, conv_b_in, conv_w_dw, conv_b_dw,
           conv_ln_g, conv_ln_b, conv_w_out, conv_b_out, ln_mix_g, ln_mix_b, ln_ffn_g, ln_ffn_b,
           moe_w_router, moe_router_bias, moe_w_gate, moe_w_up, moe_w_down,
           shared_w_gate, shared_w_up, shared_w_down):
    batch, seq, d = x.shape
    n = batch * seq
    depth = ln_mix_g.shape[0]
    nb = seq // MOBA_BLOCK
    assert seq % MOBA_BLOCK == 0 and d % N_HEADS == 0 and depth == DEPTH

    tiles = bias_tiles(rel_bias, N_HEADS, _num_bias_tiles(nb))
    xf = x.reshape(n, d)
    xb = xf.astype(BF16)
    zero_bias = jnp.zeros((d,), F32)
    for i in range(depth):
        m = i // 2
        if i % 2 == 0:
            qkv = matmul(xb, attn_w_qkv[m].astype(BF16), BF16)
            a = moba_attention(qkv, tiles, batch, seq)
            xf, xb = proj_res_ln(a, attn_w_o[m].astype(BF16), zero_bias, xf, ln_mix_g[i], ln_mix_b[i])
        else:
            h = glu_proj(xb, conv_w_in[m].astype(BF16), conv_b_in[m])
            a = dwconv_ln_silu(h, conv_w_dw[m], conv_b_dw[m], conv_ln_g[m], conv_ln_b[m], batch, seq)
            xf, xb = proj_res_ln(a, conv_w_out[m].astype(BF16), conv_b_out[m], xf, ln_mix_g[i], ln_mix_b[i])
        xf, xb = moe_ffn_ln(xf, xb, i, moe_w_router[i], moe_router_bias[i], moe_w_gate, moe_w_up, moe_w_down,
                            shared_w_gate[i], shared_w_up[i], shared_w_down[i], ln_ffn_g[i], ln_ffn_b[i])
    return xf.reshape(batch, seq, d)
```

```python
import functools
import math

import jax
import jax.numpy as jnp
from jax import lax
from jax.experimental import pallas as pl
from jax.experimental.pallas import tpu as pltpu

N_HEADS = 16
MOBA_BLOCK = 256
MOBA_TOPK = 3
REL_BUCKETS = 32
REL_MAX_DIST = 2048
CONV_K = 31
TOP_K = 8
N_GROUPS = 8
TOPK_GROUPS = 4
ROUTED_SCALE = 2.5
DEPTH = 4
DN_ALPHA = (2 * DEPTH) ** 0.25
LN_EPS = 1e-5

LANES = 128
SUBLANES = 8
VMEM_LIMIT_BYTES = 56 * 1024 * 1024

NEG_BIG = -1e30
KV_STEP = 2
HEADS_STEP = 2

F32 = jnp.float32
BF16 = jnp.bfloat16


def _params(sem):
    return pltpu.CompilerParams(dimension_semantics=sem, vmem_limit_bytes=VMEM_LIMIT_BYTES)


def _pick(n, pref):
    t = min(pref, n)
    while n % t:
        t //= 2
    return t


def _layer_norm(z, g, b):
    mu = jnp.mean(z, axis=-1, keepdims=True)
    zc = z - mu
    var = jnp.mean(zc * zc, axis=-1, keepdims=True)
    return zc * lax.rsqrt(var + LN_EPS) * g + b


def _dot(a, b):
    return jnp.dot(a, b, preferred_element_type=F32)


def _dot_nt(a, b):
    return lax.dot_general(a, b, (((1,), (1,)), ((), ())), preferred_element_type=F32)


def _mm_kernel(x_ref, w_ref, o_ref):
    o_ref[...] = _dot(x_ref[...], w_ref[...]).astype(o_ref.dtype)


def matmul(x, w, out_dtype, tm=1024, tn=512):
    n, k = x.shape
    m = w.shape[1]
    tm, tn = _pick(n, tm), _pick(m, tn)
    return pl.pallas_call(
        _mm_kernel,
        grid=(n // tm, m // tn),
        in_specs=[pl.BlockSpec((tm, k), lambda i, j: (i, 0)),
                  pl.BlockSpec((k, tn), lambda i, j: (0, j))],
        out_specs=pl.BlockSpec((tm, tn), lambda i, j: (i, j)),
        out_shape=jax.ShapeDtypeStruct((n, m), out_dtype),
        compiler_params=_params(("parallel", "arbitrary")),
        name="matmul",
    )(x, w)


def _glu_kernel(x_ref, wa_ref, wg_ref, ba_ref, bg_ref, o_ref):
    x = x_ref[...]
    a = _dot(x, wa_ref[...]) + ba_ref[...]
    g = _dot(x, wg_ref[...]) + bg_ref[...]
    o_ref[...] = a * jax.nn.sigmoid(g)


def glu_proj(x, w, b, tm=1024, tn=256):
    n, k = x.shape
    d = w.shape[1] // 2
    tm, tn = _pick(n, tm), _pick(d, tn)
    nj = d // tn
    b2 = b.reshape(1, 2 * d)
    return pl.pallas_call(
        _glu_kernel,
        grid=(n // tm, nj),
        in_specs=[pl.BlockSpec((tm, k), lambda i, j: (i, 0)),
                  pl.BlockSpec((k, tn), lambda i, j: (0, j)),
                  pl.BlockSpec((k, tn), lambda i, j: (0, j + nj)),
                  pl.BlockSpec((1, tn), lambda i, j: (0, j)),
                  pl.BlockSpec((1, tn), lambda i, j: (0, j + nj))],
        out_specs=pl.BlockSpec((tm, tn), lambda i, j: (i, j)),
        out_shape=jax.ShapeDtypeStruct((n, d), F32),
        compiler_params=_params(("parallel", "arbitrary")),
        name="glu_proj",
    )(x, w, w, b2, b2)


def _proj_res_ln_kernel(a_ref, w_ref, bias_ref, res_ref, g_ref, b_ref, o_ref, ob_ref):
    y = _dot(a_ref[...], w_ref[...]) + bias_ref[...]
    z = _layer_norm(DN_ALPHA * res_ref[...] + y, g_ref[...], b_ref[...])
    o_ref[...] = z
    ob_ref[...] = z.astype(BF16)


def proj_res_ln(a, w, bias, res, g, b, tm=256):
    n, k = a.shape
    d = w.shape[1]
    tm = _pick(n, tm)
    row = lambda i: (i, 0)
    fix = lambda i: (0, 0)
    return pl.pallas_call(
        _proj_res_ln_kernel,
        grid=(n // tm,),
        in_specs=[pl.BlockSpec((tm, k), row), pl.BlockSpec((k, d), fix),
                  pl.BlockSpec((1, d), fix), pl.BlockSpec((tm, d), row),
                  pl.BlockSpec((1, d), fix), pl.BlockSpec((1, d), fix)],
        out_specs=[pl.BlockSpec((tm, d), row), pl.BlockSpec((tm, d), row)],
        out_shape=[jax.ShapeDtypeStruct((n, d), F32), jax.ShapeDtypeStruct((n, d), BF16)],
        compiler_params=_params(("parallel",)),
        name="proj_res_ln",
    )(a, w, bias.reshape(1, d), res, g.reshape(1, d), b.reshape(1, d))


def _t5_bucket(rel):
    n = jnp.maximum(rel, 0)
    max_exact = REL_BUCKETS // 2
    nf = jnp.maximum(n, 1).astype(F32)
    large = max_exact + (jnp.log(nf / max_exact) / math.log(REL_MAX_DIST / max_exact)
                         * (REL_BUCKETS - max_exact)).astype(jnp.int32)
    large = jnp.minimum(large, REL_BUCKETS - 1)
    return jnp.where(n < max_exact, n, large)


def _bias_tile_kernel(rb_ref, o_ref, *, inv_scale):
    h = pl.program_id(0)
    d = pl.program_id(1)
    blk = o_ref.shape[-1]
    row = lax.broadcasted_iota(jnp.int32, (blk, blk), 0)
    col = lax.broadcasted_iota(jnp.int32, (blk, blk), 1)
    rel = d * blk + row - col
    bucket = _t5_bucket(rel)
    acc = jnp.zeros((blk, blk), F32)
    for k in range(REL_BUCKETS):
        acc = jnp.where(bucket == k, rb_ref[k, h], acc)
    o_ref[0, 0] = jnp.where(rel >= 0, acc * inv_scale, NEG_BIG)


def _num_bias_tiles(nb):
    last_start = (REL_MAX_DIST / (REL_BUCKETS // 2)) ** ((REL_BUCKETS // 2 - 1) / (REL_BUCKETS // 2)) \
        * (REL_BUCKETS // 2)
    d = 1
    while (d - 1) * MOBA_BLOCK + 1 < 1.05 * last_start + 1:
        d += 1
    return min(nb, d + 1)


def bias_tiles(rel_bias, n_heads, n_tiles, scale):
    return pl.pallas_call(
        functools.partial(_bias_tile_kernel, inv_scale=1.0 / scale),
        grid=(n_heads, n_tiles),
        in_specs=[pl.BlockSpec(memory_space=pltpu.SMEM)],
        out_specs=pl.BlockSpec((1, 1, MOBA_BLOCK, MOBA_BLOCK), lambda h, d: (h, d, 0, 0)),
        out_shape=jax.ShapeDtypeStruct((n_heads, n_tiles, MOBA_BLOCK, MOBA_BLOCK), F32),
        compiler_params=_params(("parallel", "parallel")),
        name="bias_tiles",
    )(rel_bias)


def _moba_kernel(q_ref, k_ref, v_ref, bias_ref, o_ref, kmean_sc, kaug_sc, *, nb, n_bias, scale):
    j = pl.program_id(2)
    blk = MOBA_BLOCK
    dh = q_ref.shape[2] // HEADS_STEP
    seq = k_ref.shape[1]
    c_exp = scale * math.log2(math.e)
    cw = KV_STEP * blk

    @pl.when(j == 0)
    def _():
        kmean_sc[...] = jnp.zeros_like(kmean_sc)
        for n in range(nb):
            kn = k_ref[0, n * blk:(n + 1) * blk, :].astype(F32)
            kmean_sc[n:n + 1, :] = jnp.mean(kn, axis=0, keepdims=True)
        blk_of_row = lax.broadcasted_iota(jnp.int32, (seq, LANES), 0) // blk
        blk_lane = lax.broadcasted_iota(jnp.int32, (seq, LANES), 1)
        onehot = jnp.where(blk_of_row == blk_lane, 1.0, 0.0).astype(BF16)
        for h in range(HEADS_STEP):
            kaug_sc[h, :, 0:dh] = k_ref[0, :, h * dh:(h + 1) * dh]
            kaug_sc[h, :, dh:dh + LANES] = onehot

    lane = lax.broadcasted_iota(jnp.int32, (blk, LANES), 1)
    lane_f = lane.astype(F32)
    j0 = pl.multiple_of(j * blk, blk)

    q_aug, state = [], []
    for h in range(HEADS_STEP):
        q = q_ref[0, :, h * dh:(h + 1) * dh]
        gate = _dot_nt(q, kmean_sc[:, h * dh:(h + 1) * dh].astype(BF16))
        gate = jnp.where(lane < j, gate, -jnp.inf)
        sel = jnp.zeros((blk, LANES), F32)
        for _ in range(MOBA_TOPK):
            mx = jnp.max(gate, axis=1, keepdims=True)
            first = jnp.min(jnp.where(gate == mx, lane_f, float(LANES)), axis=1, keepdims=True)
            hit = lane_f == first
            sel = jnp.where(hit, jnp.where(mx > -jnp.inf, 1.0, sel), sel)
            gate = jnp.where(hit, -jnp.inf, gate)
        selneg = jnp.where(sel > 0.0, 0.0, NEG_BIG).astype(BF16)
        q_aug.append(jnp.concatenate([q, selneg], axis=1))

        t = _dot_nt(q, k_ref[0, pl.ds(j0, blk), h * dh:(h + 1) * dh]) + bias_ref[h, 0]
        m0 = jnp.max(t, axis=1, keepdims=True)
        p = jnp.exp2((t - m0) * c_exp)
        l0 = jnp.sum(p, axis=1, keepdims=True)
        acc0 = _dot(p.astype(BF16), v_ref[0, pl.ds(j0, blk), h * dh:(h + 1) * dh])
        state.append((m0, l0, acc0))

    def body(c, carry):
        c0 = pl.multiple_of(c * cw, cw)
        out = []
        for h in range(HEADS_STEP):
            m, l, acc = carry[h]
            bias = jnp.concatenate(
                [bias_ref[h, jnp.clip(j - (c * KV_STEP + u), 0, n_bias - 1)] for u in range(KV_STEP)], axis=1)
            t = _dot_nt(q_aug[h], kaug_sc[h, pl.ds(c0, cw), :]) + bias
            m_new = jnp.maximum(m, jnp.max(t, axis=1, keepdims=True))
            alpha = jnp.exp2((m - m_new) * c_exp)
            p = jnp.exp2((t - m_new) * c_exp)
            l = alpha * l + jnp.sum(p, axis=1, keepdims=True)
            acc = alpha * acc + _dot(p.astype(BF16), v_ref[0, pl.ds(c0, cw), h * dh:(h + 1) * dh])
            out.append((m_new, l, acc))
        return tuple(out)

    state = lax.fori_loop(0, (j + KV_STEP - 1) // KV_STEP, body, tuple(state))
    o_ref[0] = jnp.concatenate([acc / l for (_, l, acc) in state], axis=1).astype(o_ref.dtype)


def moba_attention(qkv, tiles, batch, seq):
    d3 = qkv.shape[1]
    d = d3 // 3
    dh = d // N_HEADS
    nb = seq // MOBA_BLOCK
    assert nb % KV_STEP == 0 and N_HEADS % HEADS_STEP == 0 and dh == LANES
    n_bias = tiles.shape[1]
    hp = N_HEADS // HEADS_STEP
    wh = HEADS_STEP * dh
    qkv3 = qkv.reshape(batch, seq, d3)
    kern = functools.partial(_moba_kernel, nb=nb, n_bias=n_bias, scale=dh ** -0.5)
    out = pl.pallas_call(
        kern,
        grid=(batch, hp, nb),
        in_specs=[pl.BlockSpec((1, MOBA_BLOCK, wh), lambda b, h, j: (b, j, h)),
                  pl.BlockSpec((1, seq, wh), lambda b, h, j: (b, 0, hp + h)),
                  pl.BlockSpec((1, seq, wh), lambda b, h, j: (b, 0, 2 * hp + h)),
                  pl.BlockSpec((HEADS_STEP, n_bias, MOBA_BLOCK, MOBA_BLOCK), lambda b, h, j: (h, 0, 0, 0))],
        out_specs=pl.BlockSpec((1, MOBA_BLOCK, wh), lambda b, h, j: (b, j, h)),
        out_shape=jax.ShapeDtypeStruct((batch, seq, d), BF16),
        scratch_shapes=[pltpu.VMEM((LANES, wh), F32), pltpu.VMEM((HEADS_STEP, seq, dh + LANES), BF16)],
        compiler_params=_params(("parallel", "parallel", "arbitrary")),
        name="moba_attention",
    )(qkv3, qkv3, qkv3, tiles)
    return out.reshape(batch * seq, d)


def _dwconv_ln_kernel(prev_ref, cur_ref, w_ref, bdw_ref, g_ref, b_ref, o_ref, buf, conv_sc, hp_sc, *, halo):
    i = pl.program_id(1)
    ts, d = cur_ref.shape[1], cur_ref.shape[2]
    prev = prev_ref[0]
    buf[0:halo, :] = jnp.where(i > 0, prev, jnp.zeros_like(prev))
    buf[halo:halo + ts, :] = cur_ref[0]
    first = halo - (CONV_K - 1)
    for c in range(d // LANES):
        cs = slice(c * LANES, (c + 1) * LANES)
        acc = jnp.zeros((ts, LANES), F32) + bdw_ref[:, cs]
        for p in range(SUBLANES):
            span = (ts + halo - p) // SUBLANES * SUBLANES
            hp_sc[0:span, :] = buf[p:p + span, cs]
            for a in range(span // SUBLANES):
                k = a * SUBLANES + p - first
                if 0 <= k < CONV_K and a * SUBLANES + ts <= span:
                    acc = acc + hp_sc[a * SUBLANES:a * SUBLANES + ts, :] * w_ref[k:k + 1, cs]
        conv_sc[:, cs] = acc
    y = _layer_norm(conv_sc[...], g_ref[...], b_ref[...])
    o_ref[0] = (y * jax.nn.sigmoid(y)).astype(o_ref.dtype)


def dwconv_ln_silu(h, w_dw, b_dw, g, b, batch, seq, ts=256):
    d = h.shape[1]
    ts = _pick(seq, ts)
    halo = 32
    assert halo >= CONV_K - 1 and ts % halo == 0 and d % LANES == 0
    r = ts // halo
    h3 = h.reshape(batch, seq, d)
    fix = lambda bi, i: (0, 0)
    kern = functools.partial(_dwconv_ln_kernel, halo=halo)
    out = pl.pallas_call(
        kern,
        grid=(batch, seq // ts),
        in_specs=[pl.BlockSpec((1, halo, d), lambda bi, i: (bi, jnp.maximum(i * r - 1, 0), 0)),
                  pl.BlockSpec((1, ts, d), lambda bi, i: (bi, i, 0)),
                  pl.BlockSpec((CONV_K, d), fix), pl.BlockSpec((1, d), fix),
                  pl.BlockSpec((1, d), fix), pl.BlockSpec((1, d), fix)],
        out_specs=pl.BlockSpec((1, ts, d), lambda bi, i: (bi, i, 0)),
        out_shape=jax.ShapeDtypeStruct((batch, seq, d), BF16),
        scratch_shapes=[pltpu.VMEM((halo + ts, d), F32), pltpu.VMEM((ts, d), F32),
                        pltpu.VMEM((halo + ts, LANES), F32)],
        compiler_params=_params(("parallel", "parallel")),
        name="dwconv_ln_silu",
    )(h3, h3, w_dw, b_dw.reshape(1, d), g.reshape(1, d), b.reshape(1, d))
    return out.reshape(batch * seq, d)


def _router_kernel(x_ref, wr_ref, rb_ref, idx_ref, wt_ref, rank_ref, cnt_ref, carry, *, n_exp):
    step = pl.program_id(0)
    tm = x_ref.shape[0]
    per_g = n_exp // N_GROUPS
    lane = lax.broadcasted_iota(jnp.int32, (tm, LANES), 1)
    lane_f = lane.astype(F32)
    grp_f = (lane // per_g).astype(F32)
    valid = lane < n_exp
    logits = jnp.dot(x_ref[...], wr_ref[...], preferred_element_type=F32,
                     precision=lax.Precision.HIGHEST)
    scores = jax.nn.sigmoid(logits)
    biased = jnp.where(valid, scores + rb_ref[...], -jnp.inf)

    def first_argmax(v):
        mx = jnp.max(v, axis=1, keepdims=True)
        first = jnp.min(jnp.where(v == mx, lane_f, float(LANES)), axis=1, keepdims=True)
        return mx, first

    gscore = jnp.full((tm, LANES), -jnp.inf, F32)
    for g in range(N_GROUPS):
        vg = jnp.where(grp_f == float(g), biased, -jnp.inf)
        m1, f1 = first_argmax(vg)
        m2 = jnp.max(jnp.where(lane_f == f1, -jnp.inf, vg), axis=1, keepdims=True)
        gscore = jnp.where(lane == g, m1 + m2, gscore)
    keep = jnp.zeros((tm, LANES), F32)
    for _ in range(TOPK_GROUPS):
        _, fg = first_argmax(gscore)
        keep = jnp.where(grp_f == fg, 1.0, keep)
        gscore = jnp.where(lane_f == fg, -jnp.inf, gscore)
    cand = jnp.where(keep > 0.0, biased, -jnp.inf)
    idx = jnp.zeros((tm, LANES), F32)
    wts = jnp.zeros((tm, LANES), F32)
    chosen = jnp.zeros((tm, LANES), F32)
    picks = []
    for k in range(TOP_K):
        _, fe = first_argmax(cand)
        hit = lane_f == fe
        wk = jnp.sum(jnp.where(hit, scores, 0.0), axis=1, keepdims=True)
        idx = jnp.where(lane == k, fe, idx)
        wts = jnp.where(lane == k, wk, wts)
        chosen = jnp.where(hit, 1.0, chosen)
        cand = jnp.where(hit, -jnp.inf, cand)
        picks.append(fe)
    wsum = jnp.sum(wts, axis=1, keepdims=True)
    idx_ref[...] = idx.astype(jnp.int32)
    wt_ref[...] = wts / wsum * ROUTED_SCALE

    @pl.when(step == 0)
    def _():
        carry[...] = jnp.zeros_like(carry)

    r_i = lax.broadcasted_iota(jnp.int32, (tm, tm), 0)
    c_i = lax.broadcasted_iota(jnp.int32, (tm, tm), 1)
    earlier = jnp.where(r_i > c_i, 1.0, 0.0).astype(BF16)
    rank = _dot(earlier, chosen.astype(BF16)) + carry[...]
    rank_sel = jnp.zeros((tm, LANES), F32)
    for k in range(TOP_K):
        rk = jnp.sum(jnp.where(lane_f == picks[k], rank, 0.0), axis=1, keepdims=True)
        rank_sel = jnp.where(lane == k, rk, rank_sel)
    rank_ref[...] = rank_sel.astype(jnp.int32)
    total = carry[...] + jnp.sum(chosen, axis=0, keepdims=True)
    carry[...] = total
    cnt_ref[...] = total.astype(jnp.int32)


def router(x, w_router, r_bias, tm=256):
    n, d = x.shape
    n_exp = w_router.shape[1]
    assert n_exp <= LANES and n_exp % N_GROUPS == 0
    tm = _pick(n, tm)
    wr = jnp.pad(w_router, ((0, 0), (0, LANES - n_exp)))
    rb = jnp.pad(r_bias, (0, LANES - n_exp)).reshape(1, LANES)
    row = lambda i: (i, 0)
    fix = lambda i: (0, 0)
    return pl.pallas_call(
        functools.partial(_router_kernel, n_exp=n_exp),
        grid=(n // tm,),
        in_specs=[pl.BlockSpec((tm, d), row), pl.BlockSpec((d, LANES), fix), pl.BlockSpec((1, LANES), fix)],
        out_specs=[pl.BlockSpec((tm, LANES), row), pl.BlockSpec((tm, LANES), row),
                   pl.BlockSpec((tm, LANES), row), pl.BlockSpec((1, LANES), fix)],
        out_shape=[jax.ShapeDtypeStruct((n, LANES), jnp.int32), jax.ShapeDtypeStruct((n, LANES), F32),
                   jax.ShapeDtypeStruct((n, LANES), jnp.int32), jax.ShapeDtypeStruct((1, LANES), jnp.int32)],
        scratch_shapes=[pltpu.VMEM((1, LANES), F32)],
        compiler_params=_params(("arbitrary",)),
        name="router",
    )(x, wr, rb)


def expert_plan(idx, rank, counts, n_exp, tm):
    n_tok = idx.shape[0]
    n_asg = n_tok * TOP_K
    n_tiles = n_asg // tm
    n_items = n_tiles + n_exp - 1
    counts = counts[0, :n_exp]
    ends = jnp.cumsum(counts)
    starts = ends - counts
    e_ids = jnp.arange(n_exp, dtype=jnp.int32)
    idx_k = idx[:, :TOP_K]
    pos = rank[:, :TOP_K] + jnp.sum(jnp.where(idx_k[..., None] == e_ids, starts, 0), axis=-1)

    first_blk = starts // tm
    n_items_e = jnp.where(counts > 0, (ends - 1) // tm - first_blk + 1, 0)
    item_end = jnp.cumsum(n_items_e)
    item_start = item_end - n_items_e
    n_live = item_end[-1]
    it = jnp.minimum(jnp.arange(n_items, dtype=jnp.int32), n_live - 1)
    onehot = (jnp.sum(item_end[None, :] <= it[:, None], axis=1)[:, None] == e_ids).astype(jnp.int32)
    pick = lambda v: jnp.sum(onehot * v[None, :], axis=1)
    item_exp = pick(e_ids)
    item_blk = jnp.clip(pick(first_blk) + it - pick(item_start), 0, n_tiles - 1)
    lo = jnp.clip(pick(starts) - item_blk * tm, 0, tm)
    hi = jnp.clip(pick(ends) - item_blk * tm, 0, tm)
    one = jnp.ones((1,), jnp.int32)
    first = jnp.concatenate([one, (item_blk[1:] != item_blk[:-1]).astype(jnp.int32)])
    new_exp = jnp.concatenate([one, (item_exp[1:] != item_exp[:-1]).astype(jnp.int32)])
    tables = tuple(t.astype(jnp.int32)
                   for t in (item_blk, item_exp, lo, hi, first, new_exp, n_live.reshape(1)))
    return pos.astype(jnp.int32), tables


def _dispatch_kernel(pos_ref, x_ref, xs_hbm, sem):
    tt = x_ref.shape[0]

    def token(t, c):
        for k in range(TOP_K):
            pltpu.make_async_copy(x_ref.at[pl.ds(t, 1)], xs_hbm.at[pl.ds(pos_ref[0, 0, t * TOP_K + k], 1)],
                                  sem).start()
        return c
    lax.fori_loop(0, tt, token, 0)
    for _ in range(TOP_K):
        pltpu.make_async_copy(x_ref, xs_hbm.at[pl.ds(0, tt)], sem).wait()


def dispatch(x, pos, tt=256):
    n, d = x.shape
    tt = _pick(n, tt)
    pos3 = pos.reshape(n // tt, 1, tt * TOP_K)
    return pl.pallas_call(
        _dispatch_kernel,
        grid=(n // tt,),
        in_specs=[pl.BlockSpec((1, 1, tt * TOP_K), lambda i: (i, 0, 0), memory_space=pltpu.SMEM),
                  pl.BlockSpec((tt, d), lambda i: (i, 0))],
        out_specs=pl.BlockSpec(memory_space=pl.ANY),
        out_shape=jax.ShapeDtypeStruct((n * TOP_K, d), x.dtype),
        scratch_shapes=[pltpu.SemaphoreType.DMA],
        compiler_params=_params(("arbitrary",)),
        name="dispatch",
    )(pos3, x)


def _experts_kernel(blk_ref, exp_ref, lo_ref, hi_ref, first_ref, newexp_ref, nlive_ref,
                    xs_ref, wg_ref, wu_ref, wd_ref, y_ref, wgb, wub, wdb):
    i = pl.program_id(0)
    tm = xs_ref.shape[0]

    @pl.when(i < nlive_ref[0])
    def _():
        @pl.when(newexp_ref[i] == 1)
        def _():
            wgb[...] = wg_ref[...].astype(BF16)
            wub[...] = wu_ref[...].astype(BF16)
            wdb[...] = wd_ref[...].astype(BF16)

        x = xs_ref[...].astype(BF16)
        g = _dot(x, wgb[...])
        u = _dot(x, wub[...])
        h = (g * jax.nn.sigmoid(g) * u).astype(BF16)
        y = _dot(h, wdb[...])
        row = lax.broadcasted_iota(jnp.int32, (tm, 1), 0)
        y = jnp.where((row >= lo_ref[i]) & (row < hi_ref[i]), y, 0.0)

        @pl.when(first_ref[i] == 1)
        def _():
            y_ref[...] = y

        @pl.when(first_ref[i] == 0)
        def _():
            y_ref[...] += y


def routed_experts(xs, tables, w_gate, w_up, w_down, layer, tm):
    n_asg, d = xs.shape
    n_exp, f = w_gate.shape[1], w_gate.shape[3]
    n_items = n_asg // tm + n_exp - 1
    w_in = pl.BlockSpec((None, None, d, f), lambda i, blk, ex, *_: (layer, ex[i], 0, 0))
    w_out = pl.BlockSpec((None, None, f, d), lambda i, blk, ex, *_: (layer, ex[i], 0, 0))
    rows = pl.BlockSpec((tm, d), lambda i, blk, *_: (blk[i], 0))
    grid_spec = pltpu.PrefetchScalarGridSpec(
        num_scalar_prefetch=7,
        grid=(n_items,),
        in_specs=[rows, w_in, w_in, w_out],
        out_specs=rows,
        scratch_shapes=[pltpu.VMEM((d, f), BF16), pltpu.VMEM((d, f), BF16), pltpu.VMEM((f, d), BF16)],
    )
    return pl.pallas_call(
        _experts_kernel,
        grid_spec=grid_spec,
        out_shape=jax.ShapeDtypeStruct((n_asg, d), F32),
        compiler_params=_params(("arbitrary",)),
        name="routed_experts",
    )(*tables, xs, w_gate, w_up, w_down)


def _combine_kernel(pos_ref, y_hbm, wt_ref, xb_ref, sg_ref, su_ref, sd_ref, res_ref, g_ref, b_ref,
                    o_ref, ob_ref, ybuf, sem):
    tc = res_ref.shape[0]

    def token(t, c):
        for k in range(TOP_K):
            pltpu.make_async_copy(y_hbm.at[pl.ds(pos_ref[0, 0, t * TOP_K + k], 1)],
                                  ybuf.at[k, pl.ds(t, 1)], sem).start()
        return c
    lax.fori_loop(0, tc, token, 0)

    xb = xb_ref[...]
    hg = _dot(xb, sg_ref[...])
    hu = _dot(xb, su_ref[...])
    shared = _dot((hg * jax.nn.sigmoid(hg) * hu).astype(BF16), sd_ref[...])
    base = DN_ALPHA * res_ref[...] + shared

    for k in range(TOP_K):
        pltpu.make_async_copy(y_hbm.at[pl.ds(0, tc)], ybuf.at[k], sem).wait()
    wts = wt_ref[...]
    routed = ybuf[0] * wts[:, 0:1]
    for k in range(1, TOP_K):
        routed = routed + ybuf[k] * wts[:, k:k + 1]
    z = _layer_norm(base + routed, g_ref[...], b_ref[...])
    o_ref[...] = z
    ob_ref[...] = z.astype(BF16)


def combine_shared_ln(y, pos, wts, xb, s_gate, s_up, s_down, res, g, b, tc=128):
    n, d = res.shape
    f = s_gate.shape[1]
    tc = _pick(n, tc)
    pos3 = pos.reshape(n // tc, 1, tc * TOP_K)
    row = lambda i: (i, 0)
    fix = lambda i: (0, 0)
    return pl.pallas_call(
        _combine_kernel,
        grid=(n // tc,),
        in_specs=[pl.BlockSpec((1, 1, tc * TOP_K), lambda i: (i, 0, 0), memory_space=pltpu.SMEM),
                  pl.BlockSpec(memory_space=pl.ANY),
                  pl.BlockSpec((tc, LANES), row), pl.BlockSpec((tc, d), row),
                  pl.BlockSpec((d, f), fix), pl.BlockSpec((d, f), fix), pl.BlockSpec((f, d), fix),
                  pl.BlockSpec((tc, d), row), pl.BlockSpec((1, d), fix), pl.BlockSpec((1, d), fix)],
        out_specs=[pl.BlockSpec((tc, d), row), pl.BlockSpec((tc, d), row)],
        out_shape=[jax.ShapeDtypeStruct((n, d), F32), jax.ShapeDtypeStruct((n, d), BF16)],
        scratch_shapes=[pltpu.VMEM((TOP_K, tc, d), F32), pltpu.SemaphoreType.DMA],
        compiler_params=_params(("arbitrary",)),
        name="combine_shared_ln",
    )(pos3, y, wts, xb, s_gate, s_up, s_down, res, g.reshape(1, d), b.reshape(1, d))


def moe_ffn_ln(xf, xb, layer, w_router, r_bias, w_gate, w_up, w_down, s_gate, s_up, s_down, g, b, tm=256):
    n_exp = w_router.shape[1]
    tm = _pick(xf.shape[0] * TOP_K, tm)
    idx, wts, rank, counts = router(xf, w_router, r_bias)
    pos, tables = expert_plan(idx, rank, counts, n_exp, tm)
    xs = dispatch(xf, pos)
    y = routed_experts(xs, tables, w_gate, w_up, w_down, layer, tm)
    return combine_shared_ln(y, pos, wts, xb, s_gate.astype(BF16), s_up.astype(BF16), s_down.astype(BF16),
                             xf, g, b)


def kernel(x, rel_bias, attn_w_qkv, attn_w_o, conv_w_in, conv_b_in, conv_w_dw, conv_b_dw,
           conv_ln_g, conv_ln_b, conv_w_out, conv_b_out, ln_mix_g, ln_mix_b, ln_ffn_g, ln_ffn_b,
           moe_w_router, moe_router_bias, moe_w_gate, moe_w_up, moe_w_down,
           shared_w_gate, shared_w_up, shared_w_down):
    batch, seq, d = x.shape
    n = batch * seq
    depth = ln_mix_g.shape[0]
    nb = seq // MOBA_BLOCK
    assert seq % MOBA_BLOCK == 0 and d % N_HEADS == 0 and depth == DEPTH

    tiles = bias_tiles(rel_bias, N_HEADS, _num_bias_tiles(nb), (d // N_HEADS) ** -0.5)
    xf = x.reshape(n, d)
    xb = xf.astype(BF16)
    zero_bias = jnp.zeros((d,), F32)
    for i in range(depth):
        m = i // 2
        if i % 2 == 0:
            qkv = matmul(xb, attn_w_qkv[m].astype(BF16), BF16)
            a = moba_attention(qkv, tiles, batch, seq)
            xf, xb = proj_res_ln(a, attn_w_o[m].astype(BF16), zero_bias, xf, ln_mix_g[i], ln_mix_b[i])
        else:
            h = glu_proj(xb, conv_w_in[m].astype(BF16), conv_b_in[m])
            a = dwconv_ln_silu(h, conv_w_dw[m], conv_b_dw[m], conv_ln_g[m], conv_ln_b[m], batch, seq)
            xf, xb = proj_res_ln(a, conv_w_out[m].astype(BF16), conv_b_out[m], xf, ln_mix_g[i], ln_mix_b[i])
        xf, xb = moe_ffn_ln(xf, xb, i, moe_w_router[i], moe_router_bias[i], moe_w_gate, moe_w_up, moe_w_down,
                            shared_w_gate[i], shared_w_up[i], shared_w_down[i], ln_ffn_g[i], ln_ffn_b[i])
    return xf.reshape(batch, seq, d)
```

```python
import functools
import math

import jax
import jax.numpy as jnp
from jax import lax
from jax.experimental import pallas as pl
from jax.experimental.pallas import tpu as pltpu

N_HEADS = 16
MOBA_BLOCK = 256
MOBA_TOPK = 3
REL_BUCKETS = 32
REL_MAX_DIST = 2048
CONV_K = 31
TOP_K = 8
N_GROUPS = 8
TOPK_GROUPS = 4
ROUTED_SCALE = 2.5
DEPTH = 4
DN_ALPHA = (2 * DEPTH) ** 0.25
LN_EPS = 1e-5

LANES = 128
SUBLANES = 8
VMEM_LIMIT_BYTES = 56 * 1024 * 1024

NEG_BIG = -1e30
KV_STEP = 2
HEADS_STEP = 4

F32 = jnp.float32
BF16 = jnp.bfloat16


def _params(sem):
    return pltpu.CompilerParams(dimension_semantics=sem, vmem_limit_bytes=VMEM_LIMIT_BYTES)


def _pick(n, pref):
    t = min(pref, n)
    while n % t:
        t //= 2
    return t


def _layer_norm(z, g, b):
    mu = jnp.mean(z, axis=-1, keepdims=True)
    zc = z - mu
    var = jnp.mean(zc * zc, axis=-1, keepdims=True)
    return zc * lax.rsqrt(var + LN_EPS) * g + b


def _dot(a, b):
    return jnp.dot(a, b, preferred_element_type=F32)


def _dot_nt(a, b):
    return lax.dot_general(a, b, (((1,), (1,)), ((), ())), preferred_element_type=F32)


def _mm_kernel(x_ref, w_ref, o_ref):
    o_ref[...] = _dot(x_ref[...], w_ref[...]).astype(o_ref.dtype)


def matmul(x, w, out_dtype, tm=1024, tn=512):
    n, k = x.shape
    m = w.shape[1]
    tm, tn = _pick(n, tm), _pick(m, tn)
    return pl.pallas_call(
        _mm_kernel,
        grid=(n // tm, m // tn),
        in_specs=[pl.BlockSpec((tm, k), lambda i, j: (i, 0)),
                  pl.BlockSpec((k, tn), lambda i, j: (0, j))],
        out_specs=pl.BlockSpec((tm, tn), lambda i, j: (i, j)),
        out_shape=jax.ShapeDtypeStruct((n, m), out_dtype),
        compiler_params=_params(("parallel", "arbitrary")),
        name="matmul",
    )(x, w)


def _glu_kernel(x_ref, wa_ref, wg_ref, ba_ref, bg_ref, o_ref):
    x = x_ref[...]
    a = _dot(x, wa_ref[...]) + ba_ref[...]
    g = _dot(x, wg_ref[...]) + bg_ref[...]
    o_ref[...] = a * jax.nn.sigmoid(g)


def glu_proj(x, w, b, tm=1024, tn=256):
    n, k = x.shape
    d = w.shape[1] // 2
    tm, tn = _pick(n, tm), _pick(d, tn)
    nj = d // tn
    b2 = b.reshape(1, 2 * d)
    return pl.pallas_call(
        _glu_kernel,
        grid=(n // tm, nj),
        in_specs=[pl.BlockSpec((tm, k), lambda i, j: (i, 0)),
                  pl.BlockSpec((k, tn), lambda i, j: (0, j)),
                  pl.BlockSpec((k, tn), lambda i, j: (0, j + nj)),
                  pl.BlockSpec((1, tn), lambda i, j: (0, j)),
                  pl.BlockSpec((1, tn), lambda i, j: (0, j + nj))],
        out_specs=pl.BlockSpec((tm, tn), lambda i, j: (i, j)),
        out_shape=jax.ShapeDtypeStruct((n, d), F32),
        compiler_params=_params(("parallel", "arbitrary")),
        name="glu_proj",
    )(x, w, w, b2, b2)


def _proj_res_ln_kernel(a_ref, w_ref, bias_ref, res_ref, g_ref, b_ref, o_ref, ob_ref):
    y = _dot(a_ref[...], w_ref[...]) + bias_ref[...]
    z = _layer_norm(DN_ALPHA * res_ref[...] + y, g_ref[...], b_ref[...])
    o_ref[...] = z
    ob_ref[...] = z.astype(BF16)


def proj_res_ln(a, w, bias, res, g, b, tm=256):
    n, k = a.shape
    d = w.shape[1]
    tm = _pick(n, tm)
    row = lambda i: (i, 0)
    fix = lambda i: (0, 0)
    return pl.pallas_call(
        _proj_res_ln_kernel,
        grid=(n // tm,),
        in_specs=[pl.BlockSpec((tm, k), row), pl.BlockSpec((k, d), fix),
                  pl.BlockSpec((1, d), fix), pl.BlockSpec((tm, d), row),
                  pl.BlockSpec((1, d), fix), pl.BlockSpec((1, d), fix)],
        out_specs=[pl.BlockSpec((tm, d), row), pl.BlockSpec((tm, d), row)],
        out_shape=[jax.ShapeDtypeStruct((n, d), F32), jax.ShapeDtypeStruct((n, d), BF16)],
        compiler_params=_params(("parallel",)),
        name="proj_res_ln",
    )(a, w, bias.reshape(1, d), res, g.reshape(1, d), b.reshape(1, d))


def _t5_bucket(rel):
    n = jnp.maximum(rel, 0)
    max_exact = REL_BUCKETS // 2
    nf = jnp.maximum(n, 1).astype(F32)
    large = max_exact + (jnp.log(nf / max_exact) / math.log(REL_MAX_DIST / max_exact)
                         * (REL_BUCKETS - max_exact)).astype(jnp.int32)
    large = jnp.minimum(large, REL_BUCKETS - 1)
    return jnp.where(n < max_exact, n, large)


def _bias_tile_kernel(rb_ref, o_ref, *, inv_scale):
    h = pl.program_id(0)
    d = pl.program_id(1)
    blk = o_ref.shape[-1]
    row = lax.broadcasted_iota(jnp.int32, (blk, blk), 0)
    col = lax.broadcasted_iota(jnp.int32, (blk, blk), 1)
    rel = d * blk + row - col
    bucket = _t5_bucket(rel)
    acc = jnp.zeros((blk, blk), F32)
    for k in range(REL_BUCKETS):
        acc = jnp.where(bucket == k, rb_ref[k, h], acc)
    o_ref[0, 0] = jnp.where(rel >= 0, acc * inv_scale, NEG_BIG)


def _num_bias_tiles(nb):
    last_start = (REL_MAX_DIST / (REL_BUCKETS // 2)) ** ((REL_BUCKETS // 2 - 1) / (REL_BUCKETS // 2)) \
        * (REL_BUCKETS // 2)
    d = 1
    while (d - 1) * MOBA_BLOCK + 1 < 1.05 * last_start + 1:
        d += 1
    return min(nb, d + 1)


def bias_tiles(rel_bias, n_heads, n_tiles, scale):
    return pl.pallas_call(
        functools.partial(_bias_tile_kernel, inv_scale=1.0 / scale),
        grid=(n_heads, n_tiles),
        in_specs=[pl.BlockSpec(memory_space=pltpu.SMEM)],
        out_specs=pl.BlockSpec((1, 1, MOBA_BLOCK, MOBA_BLOCK), lambda h, d: (h, d, 0, 0)),
        out_shape=jax.ShapeDtypeStruct((n_heads, n_tiles, MOBA_BLOCK, MOBA_BLOCK), F32),
        compiler_params=_params(("parallel", "parallel")),
        name="bias_tiles",
    )(rel_bias)


def _moba_kernel(q_ref, k_ref, v_ref, bias_ref, o_ref, kmean_sc, kaug_sc, *, nb, n_bias, scale):
    j = pl.program_id(2)
    blk = MOBA_BLOCK
    dh = q_ref.shape[2] // HEADS_STEP
    seq = k_ref.shape[1]
    c_exp = scale * math.log2(math.e)
    cw = KV_STEP * blk

    @pl.when(j == 0)
    def _():
        kmean_sc[...] = jnp.zeros_like(kmean_sc)
        for n in range(nb):
            kn = k_ref[0, n * blk:(n + 1) * blk, :].astype(F32)
            kmean_sc[n:n + 1, :] = jnp.mean(kn, axis=0, keepdims=True)
        blk_of_row = lax.broadcasted_iota(jnp.int32, (seq, LANES), 0) // blk
        blk_lane = lax.broadcasted_iota(jnp.int32, (seq, LANES), 1)
        onehot = jnp.where(blk_of_row == blk_lane, 1.0, 0.0).astype(BF16)
        for h in range(HEADS_STEP):
            kaug_sc[h, :, 0:dh] = k_ref[0, :, h * dh:(h + 1) * dh]
            kaug_sc[h, :, dh:dh + LANES] = onehot

    lane = lax.broadcasted_iota(jnp.int32, (blk, LANES), 1)
    lane_f = lane.astype(F32)
    j0 = pl.multiple_of(j * blk, blk)

    q_aug, state = [], []
    for h in range(HEADS_STEP):
        q = q_ref[0, :, h * dh:(h + 1) * dh]
        gate = _dot_nt(q, kmean_sc[:, h * dh:(h + 1) * dh].astype(BF16))
        gate = jnp.where(lane < j, gate, -jnp.inf)
        sel = jnp.zeros((blk, LANES), F32)
        for _ in range(MOBA_TOPK):
            mx = jnp.max(gate, axis=1, keepdims=True)
            first = jnp.min(jnp.where(gate == mx, lane_f, float(LANES)), axis=1, keepdims=True)
            hit = lane_f == first
            sel = jnp.where(hit, jnp.where(mx > -jnp.inf, 1.0, sel), sel)
            gate = jnp.where(hit, -jnp.inf, gate)
        selneg = jnp.where(sel > 0.0, 0.0, NEG_BIG).astype(BF16)
        q_aug.append(jnp.concatenate([q, selneg], axis=1))

        t = _dot_nt(q, k_ref[0, pl.ds(j0, blk), h * dh:(h + 1) * dh]) + bias_ref[h, 0]
        m0 = jnp.max(t, axis=1, keepdims=True)
        p = jnp.exp2((t - m0) * c_exp)
        l0 = jnp.sum(p, axis=1, keepdims=True)
        acc0 = _dot(p.astype(BF16), v_ref[0, pl.ds(j0, blk), h * dh:(h + 1) * dh])
        state.append((m0, l0, acc0))

    def body(c, carry):
        c0 = pl.multiple_of(c * cw, cw)
        out = []
        for h in range(HEADS_STEP):
            m, l, acc = carry[h]
            bias = jnp.concatenate(
                [bias_ref[h, jnp.clip(j - (c * KV_STEP + u), 0, n_bias - 1)] for u in range(KV_STEP)], axis=1)
            t = _dot_nt(q_aug[h], kaug_sc[h, pl.ds(c0, cw), :]) + bias
            m_new = jnp.maximum(m, jnp.max(t, axis=1, keepdims=True))
            alpha = jnp.exp2((m - m_new) * c_exp)
            p = jnp.exp2((t - m_new) * c_exp)
            l = alpha * l + jnp.sum(p, axis=1, keepdims=True)
            acc = alpha * acc + _dot(p.astype(BF16), v_ref[0, pl.ds(c0, cw), h * dh:(h + 1) * dh])
            out.append((m_new, l, acc))
        return tuple(out)

    state = lax.fori_loop(0, (j + KV_STEP - 1) // KV_STEP, body, tuple(state))
    o_ref[0] = jnp.concatenate([acc / l for (_, l, acc) in state], axis=1).astype(o_ref.dtype)


def moba_attention(qkv, tiles, batch, seq):
    d3 = qkv.shape[1]
    d = d3 // 3
    dh = d // N_HEADS
    nb = seq // MOBA_BLOCK
    assert nb % KV_STEP == 0 and N_HEADS % HEADS_STEP == 0 and dh == LANES
    n_bias = tiles.shape[1]
    hp = N_HEADS // HEADS_STEP
    wh = HEADS_STEP * dh
    qkv3 = qkv.reshape(batch, seq, d3)
    kern = functools.partial(_moba_kernel, nb=nb, n_bias=n_bias, scale=dh ** -0.5)
    out = pl.pallas_call(
        kern,
        grid=(batch, hp, nb),
        in_specs=[pl.BlockSpec((1, MOBA_BLOCK, wh), lambda b, h, j: (b, j, h)),
                  pl.BlockSpec((1, seq, wh), lambda b, h, j: (b, 0, hp + h)),
                  pl.BlockSpec((1, seq, wh), lambda b, h, j: (b, 0, 2 * hp + h)),
                  pl.BlockSpec((HEADS_STEP, n_bias, MOBA_BLOCK, MOBA_BLOCK), lambda b, h, j: (h, 0, 0, 0))],
        out_specs=pl.BlockSpec((1, MOBA_BLOCK, wh), lambda b, h, j: (b, j, h)),
        out_shape=jax.ShapeDtypeStruct((batch, seq, d), BF16),
        scratch_shapes=[pltpu.VMEM((LANES, wh), F32), pltpu.VMEM((HEADS_STEP, seq, dh + LANES), BF16)],
        compiler_params=_params(("parallel", "parallel", "arbitrary")),
        name="moba_attention",
    )(qkv3, qkv3, qkv3, tiles)
    return out.reshape(batch * seq, d)


def _dwconv_ln_kernel(prev_ref, cur_ref, w_ref, bdw_ref, g_ref, b_ref, o_ref, buf, conv_sc, hp_sc, *, halo):
    i = pl.program_id(1)
    ts, d = cur_ref.shape[1], cur_ref.shape[2]
    prev = prev_ref[0]
    buf[0:halo, :] = jnp.where(i > 0, prev, jnp.zeros_like(prev))
    buf[halo:halo + ts, :] = cur_ref[0]
    first = halo - (CONV_K - 1)
    for c in range(d // LANES):
        cs = slice(c * LANES, (c + 1) * LANES)
        acc = jnp.zeros((ts, LANES), F32) + bdw_ref[:, cs]
        for p in range(SUBLANES):
            span = (ts + halo - p) // SUBLANES * SUBLANES
            hp_sc[0:span, :] = buf[p:p + span, cs]
            for a in range(span // SUBLANES):
                k = a * SUBLANES + p - first
                if 0 <= k < CONV_K and a * SUBLANES + ts <= span:
                    acc = acc + hp_sc[a * SUBLANES:a * SUBLANES + ts, :] * w_ref[k:k + 1, cs]
        conv_sc[:, cs] = acc
    y = _layer_norm(conv_sc[...], g_ref[...], b_ref[...])
    o_ref[0] = (y * jax.nn.sigmoid(y)).astype(o_ref.dtype)


def dwconv_ln_silu(h, w_dw, b_dw, g, b, batch, seq, ts=256):
    d = h.shape[1]
    ts = _pick(seq, ts)
    halo = 32
    assert halo >= CONV_K - 1 and ts % halo == 0 and d % LANES == 0
    r = ts // halo
    h3 = h.reshape(batch, seq, d)
    fix = lambda bi, i: (0, 0)
    kern = functools.partial(_dwconv_ln_kernel, halo=halo)
    out = pl.pallas_call(
        kern,
        grid=(batch, seq // ts),
        in_specs=[pl.BlockSpec((1, halo, d), lambda bi, i: (bi, jnp.maximum(i * r - 1, 0), 0)),
                  pl.BlockSpec((1, ts, d), lambda bi, i: (bi, i, 0)),
                  pl.BlockSpec((CONV_K, d), fix), pl.BlockSpec((1, d), fix),
                  pl.BlockSpec((1, d), fix), pl.BlockSpec((1, d), fix)],
        out_specs=pl.BlockSpec((1, ts, d), lambda bi, i: (bi, i, 0)),
        out_shape=jax.ShapeDtypeStruct((batch, seq, d), BF16),
        scratch_shapes=[pltpu.VMEM((halo + ts, d), F32), pltpu.VMEM((ts, d), F32),
                        pltpu.VMEM((halo + ts, LANES), F32)],
        compiler_params=_params(("parallel", "parallel")),
        name="dwconv_ln_silu",
    )(h3, h3, w_dw, b_dw.reshape(1, d), g.reshape(1, d), b.reshape(1, d))
    return out.reshape(batch * seq, d)


def _router_kernel(x_ref, wr_ref, rb_ref, idx_ref, wt_ref, rank_ref, cnt_ref, carry, *, n_exp):
    step = pl.program_id(0)
    tm = x_ref.shape[0]
    per_g = n_exp // N_GROUPS
    lane = lax.broadcasted_iota(jnp.int32, (tm, LANES), 1)
    lane_f = lane.astype(F32)
    grp_f = (lane // per_g).astype(F32)
    valid = lane < n_exp
    logits = jnp.dot(x_ref[...], wr_ref[...], preferred_element_type=F32,
                     precision=lax.Precision.HIGHEST)
    scores = jax.nn.sigmoid(logits)
    biased = jnp.where(valid, scores + rb_ref[...], -jnp.inf)

    def first_argmax(v):
        mx = jnp.max(v, axis=1, keepdims=True)
        first = jnp.min(jnp.where(v == mx, lane_f, float(LANES)), axis=1, keepdims=True)
        return mx, first

    gscore = jnp.full((tm, LANES), -jnp.inf, F32)
    for g in range(N_GROUPS):
        vg = jnp.where(grp_f == float(g), biased, -jnp.inf)
        m1, f1 = first_argmax(vg)
        m2 = jnp.max(jnp.where(lane_f == f1, -jnp.inf, vg), axis=1, keepdims=True)
        gscore = jnp.where(lane == g, m1 + m2, gscore)
    keep = jnp.zeros((tm, LANES), F32)
    for _ in range(TOPK_GROUPS):
        _, fg = first_argmax(gscore)
        keep = jnp.where(grp_f == fg, 1.0, keep)
        gscore = jnp.where(lane_f == fg, -jnp.inf, gscore)
    cand = jnp.where(keep > 0.0, biased, -jnp.inf)
    idx = jnp.zeros((tm, LANES), F32)
    wts = jnp.zeros((tm, LANES), F32)
    chosen = jnp.zeros((tm, LANES), F32)
    picks = []
    for k in range(TOP_K):
        _, fe = first_argmax(cand)
        hit = lane_f == fe
        wk = jnp.sum(jnp.where(hit, scores, 0.0), axis=1, keepdims=True)
        idx = jnp.where(lane == k, fe, idx)
        wts = jnp.where(lane == k, wk, wts)
        chosen = jnp.where(hit, 1.0, chosen)
        cand = jnp.where(hit, -jnp.inf, cand)
        picks.append(fe)
    wsum = jnp.sum(wts, axis=1, keepdims=True)
    idx_ref[...] = idx.astype(jnp.int32)
    wt_ref[...] = wts / wsum * ROUTED_SCALE

    @pl.when(step == 0)
    def _():
        carry[...] = jnp.zeros_like(carry)

    r_i = lax.broadcasted_iota(jnp.int32, (tm, tm), 0)
    c_i = lax.broadcasted_iota(jnp.int32, (tm, tm), 1)
    earlier = jnp.where(r_i > c_i, 1.0, 0.0).astype(BF16)
    rank = _dot(earlier, chosen.astype(BF16)) + carry[...]
    rank_sel = jnp.zeros((tm, LANES), F32)
    for k in range(TOP_K):
        rk = jnp.sum(jnp.where(lane_f == picks[k], rank, 0.0), axis=1, keepdims=True)
        rank_sel = jnp.where(lane == k, rk, rank_sel)
    rank_ref[...] = rank_sel.astype(jnp.int32)
    total = carry[...] + jnp.sum(chosen, axis=0, keepdims=True)
    carry[...] = total
    cnt_ref[...] = total.astype(jnp.int32)


def router(x, w_router, r_bias, tm=256):
    n, d = x.shape
    n_exp = w_router.shape[1]
    assert n_exp <= LANES and n_exp % N_GROUPS == 0
    tm = _pick(n, tm)
    wr = jnp.pad(w_router, ((0, 0), (0, LANES - n_exp)))
    rb = jnp.pad(r_bias, (0, LANES - n_exp)).reshape(1, LANES)
    row = lambda i: (i, 0)
    fix = lambda i: (0, 0)
    return pl.pallas_call(
        functools.partial(_router_kernel, n_exp=n_exp),
        grid=(n // tm,),
        in_specs=[pl.BlockSpec((tm, d), row), pl.BlockSpec((d, LANES), fix), pl.BlockSpec((1, LANES), fix)],
        out_specs=[pl.BlockSpec((tm, LANES), row), pl.BlockSpec((tm, LANES), row),
                   pl.BlockSpec((tm, LANES), row), pl.BlockSpec((1, LANES), fix)],
        out_shape=[jax.ShapeDtypeStruct((n, LANES), jnp.int32), jax.ShapeDtypeStruct((n, LANES), F32),
                   jax.ShapeDtypeStruct((n, LANES), jnp.int32), jax.ShapeDtypeStruct((1, LANES), jnp.int32)],
        scratch_shapes=[pltpu.VMEM((1, LANES), F32)],
        compiler_params=_params(("arbitrary",)),
        name="router",
    )(x, wr, rb)


def expert_plan(idx, rank, counts, n_exp, tm):
    n_tok = idx.shape[0]
    n_asg = n_tok * TOP_K
    n_tiles = n_asg // tm
    n_items = n_tiles + n_exp - 1
    counts = counts[0, :n_exp]
    ends = jnp.cumsum(counts)
    starts = ends - counts
    e_ids = jnp.arange(n_exp, dtype=jnp.int32)
    idx_k = idx[:, :TOP_K]
    pos = rank[:, :TOP_K] + jnp.sum(jnp.where(idx_k[..., None] == e_ids, starts, 0), axis=-1)

    first_blk = starts // tm
    n_items_e = jnp.where(counts > 0, (ends - 1) // tm - first_blk + 1, 0)
    item_end = jnp.cumsum(n_items_e)
    item_start = item_end - n_items_e
    n_live = item_end[-1]
    it = jnp.minimum(jnp.arange(n_items, dtype=jnp.int32), n_live - 1)
    onehot = (jnp.sum(item_end[None, :] <= it[:, None], axis=1)[:, None] == e_ids).astype(jnp.int32)
    pick = lambda v: jnp.sum(onehot * v[None, :], axis=1)
    item_exp = pick(e_ids)
    item_blk = jnp.clip(pick(first_blk) + it - pick(item_start), 0, n_tiles - 1)
    lo = jnp.clip(pick(starts) - item_blk * tm, 0, tm)
    hi = jnp.clip(pick(ends) - item_blk * tm, 0, tm)
    one = jnp.ones((1,), jnp.int32)
    first = jnp.concatenate([one, (item_blk[1:] != item_blk[:-1]).astype(jnp.int32)])
    new_exp = jnp.concatenate([one, (item_exp[1:] != item_exp[:-1]).astype(jnp.int32)])
    tables = tuple(t.astype(jnp.int32)
                   for t in (item_blk, item_exp, lo, hi, first, new_exp, n_live.reshape(1)))
    return pos.astype(jnp.int32), tables


def _dispatch_kernel(pos_ref, x_ref, xs_hbm, sem):
    tt = x_ref.shape[0]

    def token(t, c):
        for k in range(TOP_K):
            pltpu.make_async_copy(x_ref.at[pl.ds(t, 1)], xs_hbm.at[pl.ds(pos_ref[0, 0, t * TOP_K + k], 1)],
                                  sem).start(priority=k % 2)
        return c
    lax.fori_loop(0, tt, token, 0)
    for _ in range(TOP_K):
        pltpu.make_async_copy(x_ref, xs_hbm.at[pl.ds(0, tt)], sem).wait()


def dispatch(x, pos, tt=256):
    n, d = x.shape
    tt = _pick(n, tt)
    pos3 = pos.reshape(n // tt, 1, tt * TOP_K)
    return pl.pallas_call(
        _dispatch_kernel,
        grid=(n // tt,),
        in_specs=[pl.BlockSpec((1, 1, tt * TOP_K), lambda i: (i, 0, 0), memory_space=pltpu.SMEM),
                  pl.BlockSpec((tt, d), lambda i: (i, 0))],
        out_specs=pl.BlockSpec(memory_space=pl.ANY),
        out_shape=jax.ShapeDtypeStruct((n * TOP_K, d), x.dtype),
        scratch_shapes=[pltpu.SemaphoreType.DMA],
        compiler_params=_params(("arbitrary",)),
        name="dispatch",
    )(pos3, x)


def _experts_kernel(blk_ref, exp_ref, lo_ref, hi_ref, first_ref, newexp_ref, nlive_ref,
                    xs_ref, wg_ref, wu_ref, wd_ref, y_ref, wgb, wub, wdb):
    i = pl.program_id(0)
    tm = xs_ref.shape[0]

    @pl.when(i < nlive_ref[0])
    def _():
        @pl.when(newexp_ref[i] == 1)
        def _():
            wgb[...] = wg_ref[...].astype(BF16)
            wub[...] = wu_ref[...].astype(BF16)
            wdb[...] = wd_ref[...].astype(BF16)

        x = xs_ref[...].astype(BF16)
        g = _dot(x, wgb[...])
        u = _dot(x, wub[...])
        h = (g * jax.nn.sigmoid(g) * u).astype(BF16)
        y = _dot(h, wdb[...])
        row = lax.broadcasted_iota(jnp.int32, (tm, 1), 0)
        y = jnp.where((row >= lo_ref[i]) & (row < hi_ref[i]), y, 0.0)

        @pl.when(first_ref[i] == 1)
        def _():
            y_ref[...] = y

        @pl.when(first_ref[i] == 0)
        def _():
            y_ref[...] += y


def routed_experts(xs, tables, w_gate, w_up, w_down, layer, tm):
    n_asg, d = xs.shape
    n_exp, f = w_gate.shape[1], w_gate.shape[3]
    n_items = n_asg // tm + n_exp - 1
    w_in = pl.BlockSpec((None, None, d, f), lambda i, blk, ex, *_: (layer, ex[i], 0, 0))
    w_out = pl.BlockSpec((None, None, f, d), lambda i, blk, ex, *_: (layer, ex[i], 0, 0))
    rows = pl.BlockSpec((tm, d), lambda i, blk, *_: (blk[i], 0))
    grid_spec = pltpu.PrefetchScalarGridSpec(
        num_scalar_prefetch=7,
        grid=(n_items,),
        in_specs=[rows, w_in, w_in, w_out],
        out_specs=rows,
        scratch_shapes=[pltpu.VMEM((d, f), BF16), pltpu.VMEM((d, f), BF16), pltpu.VMEM((f, d), BF16)],
    )
    return pl.pallas_call(
        _experts_kernel,
        grid_spec=grid_spec,
        out_shape=jax.ShapeDtypeStruct((n_asg, d), F32),
        compiler_params=_params(("arbitrary",)),
        name="routed_experts",
    )(*tables, xs, w_gate, w_up, w_down)


def _combine_kernel(pos_ref, y_hbm, wt_ref, xb_ref, sg_ref, su_ref, sd_ref, res_ref, g_ref, b_ref,
                    o_ref, ob_ref, ybuf, sem):
    tc = res_ref.shape[0]

    def token(t, c):
        for k in range(TOP_K):
            pltpu.make_async_copy(y_hbm.at[pl.ds(pos_ref[0, 0, t * TOP_K + k], 1)],
                                  ybuf.at[k, pl.ds(t, 1)], sem).start(priority=k % 2)
        return c
    lax.fori_loop(0, tc, token, 0)

    xb = xb_ref[...]
    hg = _dot(xb, sg_ref[...])
    hu = _dot(xb, su_ref[...])
    shared = _dot((hg * jax.nn.sigmoid(hg) * hu).astype(BF16), sd_ref[...])
    base = DN_ALPHA * res_ref[...] + shared

    for k in range(TOP_K):
        pltpu.make_async_copy(y_hbm.at[pl.ds(0, tc)], ybuf.at[k], sem).wait()
    wts = wt_ref[...]
    routed = ybuf[0] * wts[:, 0:1]
    for k in range(1, TOP_K):
        routed = routed + ybuf[k] * wts[:, k:k + 1]
    z = _layer_norm(base + routed, g_ref[...], b_ref[...])
    o_ref[...] = z
    ob_ref[...] = z.astype(BF16)


def combine_shared_ln(y, pos, wts, xb, s_gate, s_up, s_down, res, g, b, tc=128):
    n, d = res.shape
    f = s_gate.shape[1]
    tc = _pick(n, tc)
    pos3 = pos.reshape(n // tc, 1, tc * TOP_K)
    row = lambda i: (i, 0)
    fix = lambda i: (0, 0)
    return pl.pallas_call(
        _combine_kernel,
        grid=(n // tc,),
        in_specs=[pl.BlockSpec((1, 1, tc * TOP_K), lambda i: (i, 0, 0), memory_space=pltpu.SMEM),
                  pl.BlockSpec(memory_space=pl.ANY),
                  pl.BlockSpec((tc, LANES), row), pl.BlockSpec((tc, d), row),
                  pl.BlockSpec((d, f), fix), pl.BlockSpec((d, f), fix), pl.BlockSpec((f, d), fix),
                  pl.BlockSpec((tc, d), row), pl.BlockSpec((1, d), fix), pl.BlockSpec((1, d), fix)],
        out_specs=[pl.BlockSpec((tc, d), row), pl.BlockSpec((tc, d), row)],
        out_shape=[jax.ShapeDtypeStruct((n, d), F32), jax.ShapeDtypeStruct((n, d), BF16)],
        scratch_shapes=[pltpu.VMEM((TOP_K, tc, d), F32), pltpu.SemaphoreType.DMA],
        compiler_params=_params(("arbitrary",)),
        name="combine_shared_ln",
    )(pos3, y, wts, xb, s_gate, s_up, s_down, res, g.reshape(1, d), b.reshape(1, d))


def moe_ffn_ln(xf, xb, layer, w_router, r_bias, w_gate, w_up, w_down, s_gate, s_up, s_down, g, b, tm=256):
    n_exp = w_router.shape[1]
    tm = _pick(xf.shape[0] * TOP_K, tm)
    idx, wts, rank, counts = router(xf, w_router, r_bias)
    pos, tables = expert_plan(idx, rank, counts, n_exp, tm)
    xs = dispatch(xf, pos)
    y = routed_experts(xs, tables, w_gate, w_up, w_down, layer, tm)
    return combine_shared_ln(y, pos, wts, xb, s_gate.astype(BF16), s_up.astype(BF16), s_down.astype(BF16),
                             xf, g, b)


def kernel(x, rel_bias, attn_w_qkv, attn_w_o, conv_w_in, conv_b_in, conv_w_dw, conv_b_dw,
           conv_ln_g, conv_ln_b, conv_w_out, conv_b_out, ln_mix_g, ln_mix_b, ln_ffn_g, ln_ffn_b,
           moe_w_router, moe_router_bias, moe_w_gate, moe_w_up, moe_w_down,
           shared_w_gate, shared_w_up, shared_w_down):
    batch, seq, d = x.shape
    n = batch * seq
    depth = ln_mix_g.shape[0]
    nb = seq // MOBA_BLOCK
    assert seq % MOBA_BLOCK == 0 and d % N_HEADS == 0 and depth == DEPTH

    tiles = bias_tiles(rel_bias, N_HEADS, _num_bias_tiles(nb), (d // N_HEADS) ** -0.5)
    xf = x.reshape(n, d)
    xb = xf.astype(BF16)
    zero_bias = jnp.zeros((d,), F32)
    for i in range(depth):
        m = i // 2
        if i % 2 == 0:
            qkv = matmul(xb, attn_w_qkv[m].astype(BF16), BF16)
            a = moba_attention(qkv, tiles, batch, seq)
            xf, xb = proj_res_ln(a, attn_w_o[m].astype(BF16), zero_bias, xf, ln_mix_g[i], ln_mix_b[i])
        else:
            h = glu_proj(xb, conv_w_in[m].astype(BF16), conv_b_in[m])
            a = dwconv_ln_silu(h, conv_w_dw[m], conv_b_dw[m], conv_ln_g[m], conv_ln_b[m], batch, seq)
            xf, xb = proj_res_ln(a, conv_w_out[m].astype(BF16), conv_b_out[m], xf, ln_mix_g[i], ln_mix_b[i])
        xf, xb = moe_ffn_ln(xf, xb, i, moe_w_router[i], moe_router_bias[i], moe_w_gate, moe_w_up, moe_w_down,
                            shared_w_gate[i], shared_w_up[i], shared_w_down[i], ln_ffn_g[i], ln_ffn_b[i])
    return xf.reshape(batch, seq, d)
```

```python
import functools
import math

import jax
import jax.numpy as jnp
from jax import lax
from jax.experimental import pallas as pl
from jax.experimental.pallas import tpu as pltpu

N_HEADS = 16
MOBA_BLOCK = 256
MOBA_TOPK = 3
REL_BUCKETS = 32
REL_MAX_DIST = 2048
CONV_K = 31
TOP_K = 8
N_GROUPS = 8
TOPK_GROUPS = 4
ROUTED_SCALE = 2.5
DEPTH = 4
DN_ALPHA = (2 * DEPTH) ** 0.25
LN_EPS = 1e-5

LANES = 128
SUBLANES = 8
VMEM_LIMIT_BYTES = 56 * 1024 * 1024

NEG_BIG = -1e30
KV_STEP = 2
HEADS_STEP = 4

F32 = jnp.float32
BF16 = jnp.bfloat16


def _params(sem):
    return pltpu.CompilerParams(dimension_semantics=sem, vmem_limit_bytes=VMEM_LIMIT_BYTES)


def _pick(n, pref):
    t = min(pref, n)
    while n % t:
        t //= 2
    return t


def _layer_norm(z, g, b):
    mu = jnp.mean(z, axis=-1, keepdims=True)
    zc = z - mu
    var = jnp.mean(zc * zc, axis=-1, keepdims=True)
    return zc * lax.rsqrt(var + LN_EPS) * g + b


def _dot(a, b):
    return jnp.dot(a, b, preferred_element_type=F32)


def _dot_nt(a, b):
    return lax.dot_general(a, b, (((1,), (1,)), ((), ())), preferred_element_type=F32)


def _mm_kernel(x_ref, w_ref, o_ref):
    o_ref[...] = _dot(x_ref[...], w_ref[...]).astype(o_ref.dtype)


def matmul(x, w, out_dtype, tm=1024, tn=512):
    n, k = x.shape
    m = w.shape[1]
    tm, tn = _pick(n, tm), _pick(m, tn)
    return pl.pallas_call(
        _mm_kernel,
        grid=(n // tm, m // tn),
        in_specs=[pl.BlockSpec((tm, k), lambda i, j: (i, 0)),
                  pl.BlockSpec((k, tn), lambda i, j: (0, j))],
        out_specs=pl.BlockSpec((tm, tn), lambda i, j: (i, j)),
        out_shape=jax.ShapeDtypeStruct((n, m), out_dtype),
        compiler_params=_params(("parallel", "arbitrary")),
        name="matmul",
    )(x, w)


def _glu_kernel(x_ref, wa_ref, wg_ref, ba_ref, bg_ref, o_ref):
    x = x_ref[...]
    a = _dot(x, wa_ref[...]) + ba_ref[...]
    g = _dot(x, wg_ref[...]) + bg_ref[...]
    o_ref[...] = a * jax.nn.sigmoid(g)


def glu_proj(x, w, b, tm=1024, tn=256):
    n, k = x.shape
    d = w.shape[1] // 2
    tm, tn = _pick(n, tm), _pick(d, tn)
    nj = d // tn
    b2 = b.reshape(1, 2 * d)
    return pl.pallas_call(
        _glu_kernel,
        grid=(n // tm, nj),
        in_specs=[pl.BlockSpec((tm, k), lambda i, j: (i, 0)),
                  pl.BlockSpec((k, tn), lambda i, j: (0, j)),
                  pl.BlockSpec((k, tn), lambda i, j: (0, j + nj)),
                  pl.BlockSpec((1, tn), lambda i, j: (0, j)),
                  pl.BlockSpec((1, tn), lambda i, j: (0, j + nj))],
        out_specs=pl.BlockSpec((tm, tn), lambda i, j: (i, j)),
        out_shape=jax.ShapeDtypeStruct((n, d), F32),
        compiler_params=_params(("parallel", "arbitrary")),
        name="glu_proj",
    )(x, w, w, b2, b2)


def _proj_res_ln_kernel(a_ref, w_ref, bias_ref, res_ref, g_ref, b_ref, o_ref, ob_ref):
    y = _dot(a_ref[...], w_ref[...]) + bias_ref[...]
    z = _layer_norm(DN_ALPHA * res_ref[...] + y, g_ref[...], b_ref[...])
    o_ref[...] = z
    ob_ref[...] = z.astype(BF16)


def proj_res_ln(a, w, bias, res, g, b, tm=256):
    n, k = a.shape
    d = w.shape[1]
    tm = _pick(n, tm)
    row = lambda i: (i, 0)
    fix = lambda i: (0, 0)
    return pl.pallas_call(
        _proj_res_ln_kernel,
        grid=(n // tm,),
        in_specs=[pl.BlockSpec((tm, k), row), pl.BlockSpec((k, d), fix),
                  pl.BlockSpec((1, d), fix), pl.BlockSpec((tm, d), row),
                  pl.BlockSpec((1, d), fix), pl.BlockSpec((1, d), fix)],
        out_specs=[pl.BlockSpec((tm, d), row), pl.BlockSpec((tm, d), row)],
        out_shape=[jax.ShapeDtypeStruct((n, d), F32), jax.ShapeDtypeStruct((n, d), BF16)],
        compiler_params=_params(("parallel",)),
        name="proj_res_ln",
    )(a, w, bias.reshape(1, d), res, g.reshape(1, d), b.reshape(1, d))


def _t5_bucket(rel):
    n = jnp.maximum(rel, 0)
    max_exact = REL_BUCKETS // 2
    nf = jnp.maximum(n, 1).astype(F32)
    large = max_exact + (jnp.log(nf / max_exact) / math.log(REL_MAX_DIST / max_exact)
                         * (REL_BUCKETS - max_exact)).astype(jnp.int32)
    large = jnp.minimum(large, REL_BUCKETS - 1)
    return jnp.where(n < max_exact, n, large)


def _bias_tile_kernel(rb_ref, o_ref, *, inv_scale):
    h = pl.program_id(0)
    d = pl.program_id(1)
    blk = o_ref.shape[-1]
    row = lax.broadcasted_iota(jnp.int32, (blk, blk), 0)
    col = lax.broadcasted_iota(jnp.int32, (blk, blk), 1)
    rel = d * blk + row - col
    bucket = _t5_bucket(rel)
    acc = jnp.zeros((blk, blk), F32)
    for k in range(REL_BUCKETS):
        acc = jnp.where(bucket == k, rb_ref[k, h], acc)
    o_ref[0, 0] = jnp.where(rel >= 0, acc * inv_scale, NEG_BIG)


def _num_bias_tiles(nb):
    last_start = (REL_MAX_DIST / (REL_BUCKETS // 2)) ** ((REL_BUCKETS // 2 - 1) / (REL_BUCKETS // 2)) \
        * (REL_BUCKETS // 2)
    d = 1
    while (d - 1) * MOBA_BLOCK + 1 < 1.05 * last_start + 1:
        d += 1
    return min(nb, d + 1)


def bias_tiles(rel_bias, n_heads, n_tiles, scale):
    return pl.pallas_call(
        functools.partial(_bias_tile_kernel, inv_scale=1.0 / scale),
        grid=(n_heads, n_tiles),
        in_specs=[pl.BlockSpec(memory_space=pltpu.SMEM)],
        out_specs=pl.BlockSpec((1, 1, MOBA_BLOCK, MOBA_BLOCK), lambda h, d: (h, d, 0, 0)),
        out_shape=jax.ShapeDtypeStruct((n_heads, n_tiles, MOBA_BLOCK, MOBA_BLOCK), F32),
        compiler_params=_params(("parallel", "parallel")),
        name="bias_tiles",
    )(rel_bias)


def _moba_kernel(q_ref, k_ref, v_ref, bias_ref, o_ref, kmean_sc, kaug_sc, *, nb, n_bias, scale):
    j = pl.program_id(2)
    blk = MOBA_BLOCK
    dh = q_ref.shape[2] // HEADS_STEP
    seq = k_ref.shape[1]
    c_exp = scale * math.log2(math.e)
    cw = KV_STEP * blk

    @pl.when(j == 0)
    def _():
        kmean_sc[...] = jnp.zeros_like(kmean_sc)
        for n in range(nb):
            kn = k_ref[0, n * blk:(n + 1) * blk, :].astype(F32)
            kmean_sc[n:n + 1, :] = jnp.mean(kn, axis=0, keepdims=True)
        blk_of_row = lax.broadcasted_iota(jnp.int32, (seq, LANES), 0) // blk
        blk_lane = lax.broadcasted_iota(jnp.int32, (seq, LANES), 1)
        onehot = jnp.where(blk_of_row == blk_lane, 1.0, 0.0).astype(BF16)
        for h in range(HEADS_STEP):
            kaug_sc[h, :, 0:dh] = k_ref[0, :, h * dh:(h + 1) * dh]
            kaug_sc[h, :, dh:dh + LANES] = onehot

    lane = lax.broadcasted_iota(jnp.int32, (blk, LANES), 1)
    lane_f = lane.astype(F32)
    j0 = pl.multiple_of(j * blk, blk)

    q_aug, state = [], []
    for h in range(HEADS_STEP):
        q = q_ref[0, :, h * dh:(h + 1) * dh]
        gate = _dot_nt(q, kmean_sc[:, h * dh:(h + 1) * dh].astype(BF16))
        gate = jnp.where(lane < j, gate, -jnp.inf)
        sel = jnp.zeros((blk, LANES), F32)
        for _ in range(MOBA_TOPK):
            mx = jnp.max(gate, axis=1, keepdims=True)
            first = jnp.min(jnp.where(gate == mx, lane_f, float(LANES)), axis=1, keepdims=True)
            hit = lane_f == first
            sel = jnp.where(hit, jnp.where(mx > -jnp.inf, 1.0, sel), sel)
            gate = jnp.where(hit, -jnp.inf, gate)
        selneg = jnp.where(sel > 0.0, 0.0, NEG_BIG).astype(BF16)
        q_aug.append(jnp.concatenate([q, selneg], axis=1))

        t = _dot_nt(q, k_ref[0, pl.ds(j0, blk), h * dh:(h + 1) * dh]) + bias_ref[h, 0]
        m0 = jnp.max(t, axis=1, keepdims=True)
        p = jnp.exp2((t - m0) * c_exp)
        l0 = jnp.sum(p, axis=1, keepdims=True)
        acc0 = _dot(p.astype(BF16), v_ref[0, pl.ds(j0, blk), h * dh:(h + 1) * dh])
        state.append((m0, l0, acc0))

    def body(c, carry):
        c0 = pl.multiple_of(c * cw, cw)
        out = []
        for h in range(HEADS_STEP):
            m, l, acc = carry[h]
            bias = jnp.concatenate(
                [bias_ref[h, jnp.clip(j - (c * KV_STEP + u), 0, n_bias - 1)] for u in range(KV_STEP)], axis=1)
            t = _dot_nt(q_aug[h], kaug_sc[h, pl.ds(c0, cw), :]) + bias
            m_new = jnp.maximum(m, jnp.max(t, axis=1, keepdims=True))
            alpha = jnp.exp2((m - m_new) * c_exp)
            p = jnp.exp2((t - m_new) * c_exp)
            l = alpha * l + jnp.sum(p, axis=1, keepdims=True)
            acc = alpha * acc + _dot(p.astype(BF16), v_ref[0, pl.ds(c0, cw), h * dh:(h + 1) * dh])
            out.append((m_new, l, acc))
        return tuple(out)

    state = lax.fori_loop(0, (j + KV_STEP - 1) // KV_STEP, body, tuple(state))
    o_ref[0] = jnp.concatenate([acc / l for (_, l, acc) in state], axis=1).astype(o_ref.dtype)


def moba_attention(qkv, tiles, batch, seq):
    d3 = qkv.shape[1]
    d = d3 // 3
    dh = d // N_HEADS
    nb = seq // MOBA_BLOCK
    assert nb % KV_STEP == 0 and N_HEADS % HEADS_STEP == 0 and dh == LANES
    n_bias = tiles.shape[1]
    hp = N_HEADS // HEADS_STEP
    wh = HEADS_STEP * dh
    qkv3 = qkv.reshape(batch, seq, d3)
    kern = functools.partial(_moba_kernel, nb=nb, n_bias=n_bias, scale=dh ** -0.5)
    out = pl.pallas_call(
        kern,
        grid=(batch, hp, nb),
        in_specs=[pl.BlockSpec((1, MOBA_BLOCK, wh), lambda b, h, j: (b, j, h)),
                  pl.BlockSpec((1, seq, wh), lambda b, h, j: (b, 0, hp + h)),
                  pl.BlockSpec((1, seq, wh), lambda b, h, j: (b, 0, 2 * hp + h)),
                  pl.BlockSpec((HEADS_STEP, n_bias, MOBA_BLOCK, MOBA_BLOCK), lambda b, h, j: (h, 0, 0, 0))],
        out_specs=pl.BlockSpec((1, MOBA_BLOCK, wh), lambda b, h, j: (b, j, h)),
        out_shape=jax.ShapeDtypeStruct((batch, seq, d), BF16),
        scratch_shapes=[pltpu.VMEM((LANES, wh), F32), pltpu.VMEM((HEADS_STEP, seq, dh + LANES), BF16)],
        compiler_params=_params(("parallel", "parallel", "arbitrary")),
        name="moba_attention",
    )(qkv3, qkv3, qkv3, tiles)
    return out.reshape(batch * seq, d)


def _dwconv_ln_kernel(prev_ref, cur_ref, w_ref, bdw_ref, g_ref, b_ref, o_ref, buf, conv_sc, hp_sc, *, halo):
    i = pl.program_id(1)
    ts, d = cur_ref.shape[1], cur_ref.shape[2]
    prev = prev_ref[0]
    buf[0:halo, :] = jnp.where(i > 0, prev, jnp.zeros_like(prev))
    buf[halo:halo + ts, :] = cur_ref[0]
    first = halo - (CONV_K - 1)
    for c in range(d // LANES):
        cs = slice(c * LANES, (c + 1) * LANES)
        acc = jnp.zeros((ts, LANES), F32) + bdw_ref[:, cs]
        for p in range(SUBLANES):
            span = (ts + halo - p) // SUBLANES * SUBLANES
            hp_sc[0:span, :] = buf[p:p + span, cs]
            for a in range(span // SUBLANES):
                k = a * SUBLANES + p - first
                if 0 <= k < CONV_K and a * SUBLANES + ts <= span:
                    acc = acc + hp_sc[a * SUBLANES:a * SUBLANES + ts, :] * w_ref[k:k + 1, cs]
        conv_sc[:, cs] = acc
    y = _layer_norm(conv_sc[...], g_ref[...], b_ref[...])
    o_ref[0] = (y * jax.nn.sigmoid(y)).astype(o_ref.dtype)


def dwconv_ln_silu(h, w_dw, b_dw, g, b, batch, seq, ts=256):
    d = h.shape[1]
    ts = _pick(seq, ts)
    halo = 32
    assert halo >= CONV_K - 1 and ts % halo == 0 and d % LANES == 0
    r = ts // halo
    h3 = h.reshape(batch, seq, d)
    fix = lambda bi, i: (0, 0)
    kern = functools.partial(_dwconv_ln_kernel, halo=halo)
    out = pl.pallas_call(
        kern,
        grid=(batch, seq // ts),
        in_specs=[pl.BlockSpec((1, halo, d), lambda bi, i: (bi, jnp.maximum(i * r - 1, 0), 0)),
                  pl.BlockSpec((1, ts, d), lambda bi, i: (bi, i, 0)),
                  pl.BlockSpec((CONV_K, d), fix), pl.BlockSpec((1, d), fix),
                  pl.BlockSpec((1, d), fix), pl.BlockSpec((1, d), fix)],
        out_specs=pl.BlockSpec((1, ts, d), lambda bi, i: (bi, i, 0)),
        out_shape=jax.ShapeDtypeStruct((batch, seq, d), BF16),
        scratch_shapes=[pltpu.VMEM((halo + ts, d), F32), pltpu.VMEM((ts, d), F32),
                        pltpu.VMEM((halo + ts, LANES), F32)],
        compiler_params=_params(("parallel", "parallel")),
        name="dwconv_ln_silu",
    )(h3, h3, w_dw, b_dw.reshape(1, d), g.reshape(1, d), b.reshape(1, d))
    return out.reshape(batch * seq, d)


def _router_kernel(x_ref, wr_ref, rb_ref, idx_ref, wt_ref, rank_ref, cnt_ref, carry, *, n_exp):
    step = pl.program_id(0)
    tm = x_ref.shape[0]
    per_g = n_exp // N_GROUPS
    lane = lax.broadcasted_iota(jnp.int32, (tm, LANES), 1)
    lane_f = lane.astype(F32)
    grp_f = (lane // per_g).astype(F32)
    valid = lane < n_exp
    logits = _dot(x_ref[...], wr_ref[...])
    scores = jax.nn.sigmoid(logits)
    biased = jnp.where(valid, scores + rb_ref[...], -jnp.inf)

    def first_argmax(v):
        mx = jnp.max(v, axis=1, keepdims=True)
        first = jnp.min(jnp.where(v == mx, lane_f, float(LANES)), axis=1, keepdims=True)
        return mx, first

    gscore = jnp.full((tm, LANES), -jnp.inf, F32)
    for g in range(N_GROUPS):
        vg = jnp.where(grp_f == float(g), biased, -jnp.inf)
        m1, f1 = first_argmax(vg)
        m2 = jnp.max(jnp.where(lane_f == f1, -jnp.inf, vg), axis=1, keepdims=True)
        gscore = jnp.where(lane == g, m1 + m2, gscore)
    keep = jnp.zeros((tm, LANES), F32)
    for _ in range(TOPK_GROUPS):
        _, fg = first_argmax(gscore)
        keep = jnp.where(grp_f == fg, 1.0, keep)
        gscore = jnp.where(lane_f == fg, -jnp.inf, gscore)
    cand = jnp.where(keep > 0.0, biased, -jnp.inf)
    idx = jnp.zeros((tm, LANES), F32)
    wts = jnp.zeros((tm, LANES), F32)
    chosen = jnp.zeros((tm, LANES), F32)
    picks = []
    for k in range(TOP_K):
        _, fe = first_argmax(cand)
        hit = lane_f == fe
        wk = jnp.sum(jnp.where(hit, scores, 0.0), axis=1, keepdims=True)
        idx = jnp.where(lane == k, fe, idx)
        wts = jnp.where(lane == k, wk, wts)
        chosen = jnp.where(hit, 1.0, chosen)
        cand = jnp.where(hit, -jnp.inf, cand)
        picks.append(fe)
    wsum = jnp.sum(wts, axis=1, keepdims=True)
    idx_ref[...] = idx.astype(jnp.int32)
    wt_ref[...] = wts / wsum * ROUTED_SCALE

    @pl.when(step == 0)
    def _():
        carry[...] = jnp.zeros_like(carry)

    r_i = lax.broadcasted_iota(jnp.int32, (tm, tm), 0)
    c_i = lax.broadcasted_iota(jnp.int32, (tm, tm), 1)
    earlier = jnp.where(r_i > c_i, 1.0, 0.0).astype(BF16)
    rank = _dot(earlier, chosen.astype(BF16)) + carry[...]
    rank_sel = jnp.zeros((tm, LANES), F32)
    for k in range(TOP_K):
        rk = jnp.sum(jnp.where(lane_f == picks[k], rank, 0.0), axis=1, keepdims=True)
        rank_sel = jnp.where(lane == k, rk, rank_sel)
    rank_ref[...] = rank_sel.astype(jnp.int32)
    total = carry[...] + jnp.sum(chosen, axis=0, keepdims=True)
    carry[...] = total
    cnt_ref[...] = total.astype(jnp.int32)


def router(x, w_router, r_bias, tm=256):
    n, d = x.shape
    n_exp = w_router.shape[1]
    assert n_exp <= LANES and n_exp % N_GROUPS == 0
    tm = _pick(n, tm)
    wr = jnp.pad(w_router, ((0, 0), (0, LANES - n_exp))).astype(BF16)
    rb = jnp.pad(r_bias, (0, LANES - n_exp)).reshape(1, LANES)
    row = lambda i: (i, 0)
    fix = lambda i: (0, 0)
    return pl.pallas_call(
        functools.partial(_router_kernel, n_exp=n_exp),
        grid=(n // tm,),
        in_specs=[pl.BlockSpec((tm, d), row), pl.BlockSpec((d, LANES), fix), pl.BlockSpec((1, LANES), fix)],
        out_specs=[pl.BlockSpec((tm, LANES), row), pl.BlockSpec((tm, LANES), row),
                   pl.BlockSpec((tm, LANES), row), pl.BlockSpec((1, LANES), fix)],
        out_shape=[jax.ShapeDtypeStruct((n, LANES), jnp.int32), jax.ShapeDtypeStruct((n, LANES), F32),
                   jax.ShapeDtypeStruct((n, LANES), jnp.int32), jax.ShapeDtypeStruct((1, LANES), jnp.int32)],
        scratch_shapes=[pltpu.VMEM((1, LANES), F32)],
        compiler_params=_params(("arbitrary",)),
        name="router",
    )(x, wr, rb)


def expert_plan(idx, rank, counts, n_exp, tm):
    n_tok = idx.shape[0]
    n_asg = n_tok * TOP_K
    n_tiles = n_asg // tm
    n_items = n_tiles + n_exp - 1
    counts = counts[0, :n_exp]
    ends = jnp.cumsum(counts)
    starts = ends - counts
    e_ids = jnp.arange(n_exp, dtype=jnp.int32)
    idx_k = idx[:, :TOP_K]
    pos = rank[:, :TOP_K] + jnp.sum(jnp.where(idx_k[..., None] == e_ids, starts, 0), axis=-1)

    first_blk = starts // tm
    n_items_e = jnp.where(counts > 0, (ends - 1) // tm - first_blk + 1, 0)
    item_end = jnp.cumsum(n_items_e)
    item_start = item_end - n_items_e
    n_live = item_end[-1]
    it = jnp.minimum(jnp.arange(n_items, dtype=jnp.int32), n_live - 1)
    onehot = (jnp.sum(item_end[None, :] <= it[:, None], axis=1)[:, None] == e_ids).astype(jnp.int32)
    pick = lambda v: jnp.sum(onehot * v[None, :], axis=1)
    item_exp = pick(e_ids)
    item_blk = jnp.clip(pick(first_blk) + it - pick(item_start), 0, n_tiles - 1)
    lo = jnp.clip(pick(starts) - item_blk * tm, 0, tm)
    hi = jnp.clip(pick(ends) - item_blk * tm, 0, tm)
    one = jnp.ones((1,), jnp.int32)
    first = jnp.concatenate([one, (item_blk[1:] != item_blk[:-1]).astype(jnp.int32)])
    new_exp = jnp.concatenate([one, (item_exp[1:] != item_exp[:-1]).astype(jnp.int32)])
    tables = tuple(t.astype(jnp.int32)
                   for t in (item_blk, item_exp, lo, hi, first, new_exp, n_live.reshape(1)))
    return pos.astype(jnp.int32), tables


def _dispatch_kernel(pos_ref, x_ref, xs_hbm, sem):
    tt = x_ref.shape[0]
    for t in range(tt):
        for k in range(TOP_K):
            pltpu.make_async_copy(x_ref.at[pl.ds(t, 1)], xs_hbm.at[pl.ds(pos_ref[0, 0, t * TOP_K + k], 1)],
                                  sem).start(priority=k % 2)
    for _ in range(TOP_K):
        pltpu.make_async_copy(x_ref, xs_hbm.at[pl.ds(0, tt)], sem).wait()


def dispatch(x, pos, tt=128):
    n, d = x.shape
    tt = _pick(n, tt)
    pos3 = pos.reshape(n // tt, 1, tt * TOP_K)
    return pl.pallas_call(
        _dispatch_kernel,
        grid=(n // tt,),
        in_specs=[pl.BlockSpec((1, 1, tt * TOP_K), lambda i: (i, 0, 0), memory_space=pltpu.SMEM),
                  pl.BlockSpec((tt, d), lambda i: (i, 0))],
        out_specs=pl.BlockSpec(memory_space=pl.ANY),
        out_shape=jax.ShapeDtypeStruct((n * TOP_K, d), x.dtype),
        scratch_shapes=[pltpu.SemaphoreType.DMA],
        compiler_params=_params(("arbitrary",)),
        name="dispatch",
    )(pos3, x)


def _experts_kernel(blk_ref, exp_ref, lo_ref, hi_ref, first_ref, newexp_ref, nlive_ref,
                    xs_ref, wg_ref, wu_ref, wd_ref, y_ref, wgb, wub, wdb):
    i = pl.program_id(0)
    tm = xs_ref.shape[0]

    @pl.when(i < nlive_ref[0])
    def _():
        @pl.when(newexp_ref[i] == 1)
        def _():
            wgb[...] = wg_ref[...].astype(BF16)
            wub[...] = wu_ref[...].astype(BF16)
            wdb[...] = wd_ref[...].astype(BF16)

        x = xs_ref[...].astype(BF16)
        g = _dot(x, wgb[...])
        u = _dot(x, wub[...])
        h = (g * jax.nn.sigmoid(g) * u).astype(BF16)
        y = _dot(h, wdb[...])
        row = lax.broadcasted_iota(jnp.int32, (tm, 1), 0)
        y = jnp.where((row >= lo_ref[i]) & (row < hi_ref[i]), y, 0.0)

        @pl.when(first_ref[i] == 1)
        def _():
            y_ref[...] = y

        @pl.when(first_ref[i] == 0)
        def _():
            y_ref[...] += y


def routed_experts(xs, tables, w_gate, w_up, w_down, layer, tm):
    n_asg, d = xs.shape
    n_exp, f = w_gate.shape[1], w_gate.shape[3]
    n_items = n_asg // tm + n_exp - 1
    w_in = pl.BlockSpec((None, None, d, f), lambda i, blk, ex, *_: (layer, ex[i], 0, 0))
    w_out = pl.BlockSpec((None, None, f, d), lambda i, blk, ex, *_: (layer, ex[i], 0, 0))
    rows = pl.BlockSpec((tm, d), lambda i, blk, *_: (blk[i], 0))
    grid_spec = pltpu.PrefetchScalarGridSpec(
        num_scalar_prefetch=7,
        grid=(n_items,),
        in_specs=[rows, w_in, w_in, w_out],
        out_specs=rows,
        scratch_shapes=[pltpu.VMEM((d, f), BF16), pltpu.VMEM((d, f), BF16), pltpu.VMEM((f, d), BF16)],
    )
    return pl.pallas_call(
        _experts_kernel,
        grid_spec=grid_spec,
        out_shape=jax.ShapeDtypeStruct((n_asg, d), F32),
        compiler_params=_params(("arbitrary",)),
        name="routed_experts",
    )(*tables, xs, w_gate, w_up, w_down)


def _combine_kernel(pos_ref, posn_ref, y_hbm, wt_ref, xb_ref, sg_ref, su_ref, sd_ref, res_ref, g_ref, b_ref,
                    o_ref, ob_ref, ybuf, sems):
    i = pl.program_id(0)
    n_steps = pl.num_programs(0)
    tc = res_ref.shape[0]
    slot = i % 2
    nxt = 1 - slot

    def gather_wait(s):
        for k in range(TOP_K):
            pltpu.make_async_copy(y_hbm.at[pl.ds(0, tc)], ybuf.at[s, k], sems.at[s]).wait()

    @pl.when(i == 0)
    def _():
        def token(t, c):
            for k in range(TOP_K):
                pltpu.make_async_copy(y_hbm.at[pl.ds(pos_ref[0, 0, t * TOP_K + k], 1)],
                                      ybuf.at[0, k, pl.ds(t, 1)], sems.at[0]).start(priority=k % 2)
            return c
        lax.fori_loop(0, tc, token, 0)

    for t in range(tc):
        for k in range(TOP_K):
            pltpu.make_async_copy(y_hbm.at[pl.ds(posn_ref[0, 0, t * TOP_K + k], 1)],
                                  ybuf.at[nxt, k, pl.ds(t, 1)], sems.at[nxt]).start(priority=k % 2)

    xb = xb_ref[...]
    hg = _dot(xb, sg_ref[...])
    hu = _dot(xb, su_ref[...])
    shared = _dot((hg * jax.nn.sigmoid(hg) * hu).astype(BF16), sd_ref[...])
    base = DN_ALPHA * res_ref[...] + shared

    gather_wait(slot)
    wts = wt_ref[...]
    routed = ybuf[slot, 0] * wts[:, 0:1]
    for k in range(1, TOP_K):
        routed = routed + ybuf[slot, k] * wts[:, k:k + 1]
    z = _layer_norm(base + routed, g_ref[...], b_ref[...])
    o_ref[...] = z
    ob_ref[...] = z.astype(BF16)

    @pl.when(i == n_steps - 1)
    def _():
        gather_wait(nxt)


def combine_shared_ln(y, pos, wts, xb, s_gate, s_up, s_down, res, g, b, tc=128):
    n, d = res.shape
    f = s_gate.shape[1]
    tc = _pick(n, tc)
    n_steps = n // tc
    pos3 = pos.reshape(n_steps, 1, tc * TOP_K)
    row = lambda i: (i, 0)
    fix = lambda i: (0, 0)
    pos_blk = lambda imap: pl.BlockSpec((1, 1, tc * TOP_K), imap, memory_space=pltpu.SMEM)
    return pl.pallas_call(
        _combine_kernel,
        grid=(n_steps,),
        in_specs=[pos_blk(lambda i: (i, 0, 0)),
                  pos_blk(lambda i: (jnp.minimum(i + 1, n_steps - 1), 0, 0)),
                  pl.BlockSpec(memory_space=pl.ANY),
                  pl.BlockSpec((tc, LANES), row), pl.BlockSpec((tc, d), row),
                  pl.BlockSpec((d, f), fix), pl.BlockSpec((d, f), fix), pl.BlockSpec((f, d), fix),
                  pl.BlockSpec((tc, d), row), pl.BlockSpec((1, d), fix), pl.BlockSpec((1, d), fix)],
        out_specs=[pl.BlockSpec((tc, d), row), pl.BlockSpec((tc, d), row)],
        out_shape=[jax.ShapeDtypeStruct((n, d), F32), jax.ShapeDtypeStruct((n, d), BF16)],
        scratch_shapes=[pltpu.VMEM((2, TOP_K, tc, d), F32), pltpu.SemaphoreType.DMA((2,))],
        compiler_params=_params(("arbitrary",)),
        name="combine_shared_ln",
    )(pos3, pos3, y, wts, xb, s_gate, s_up, s_down, res, g.reshape(1, d), b.reshape(1, d))


def moe_ffn_ln(xf, xb, layer, w_router, r_bias, w_gate, w_up, w_down, s_gate, s_up, s_down, g, b, tm=256):
    n_exp = w_router.shape[1]
    tm = _pick(xf.shape[0] * TOP_K, tm)
    idx, wts, rank, counts = router(xb, w_router, r_bias)
    pos, tables = expert_plan(idx, rank, counts, n_exp, tm)
    xs = dispatch(xf, pos)
    y = routed_experts(xs, tables, w_gate, w_up, w_down, layer, tm)
    return combine_shared_ln(y, pos, wts, xb, s_gate.astype(BF16), s_up.astype(BF16), s_down.astype(BF16),
                             xf, g, b)


def kernel(x, rel_bias, attn_w_qkv, attn_w_o, conv_w_in, conv_b_in, conv_w_dw, conv_b_dw,
           conv_ln_g, conv_ln_b, conv_w_out, conv_b_out, ln_mix_g, ln_mix_b, ln_ffn_g, ln_ffn_b,
           moe_w_router, moe_router_bias, moe_w_gate, moe_w_up, moe_w_down,
           shared_w_gate, shared_w_up, shared_w_down):
    batch, seq, d = x.shape
    n = batch * seq
    depth = ln_mix_g.shape[0]
    nb = seq // MOBA_BLOCK
    assert seq % MOBA_BLOCK == 0 and d % N_HEADS == 0 and depth == DEPTH

    tiles = bias_tiles(rel_bias, N_HEADS, _num_bias_tiles(nb), (d // N_HEADS) ** -0.5)
    xf = x.reshape(n, d)
    xb = xf.astype(BF16)
    zero_bias = jnp.zeros((d,), F32)
    for i in range(depth):
        m = i // 2
        if i % 2 == 0:
            qkv = matmul(xb, attn_w_qkv[m].astype(BF16), BF16)
            a = moba_attention(qkv, tiles, batch, seq)
            xf, xb = proj_res_ln(a, attn_w_o[m].astype(BF16), zero_bias, xf, ln_mix_g[i], ln_mix_b[i])
        else:
            h = glu_proj(xb, conv_w_in[m].astype(BF16), conv_b_in[m])
            a = dwconv_ln_silu(h, conv_w_dw[m], conv_b_dw[m], conv_ln_g[m], conv_ln_b[m], batch, seq)
            xf, xb = proj_res_ln(a, conv_w_out[m].astype(BF16), conv_b_out[m], xf, ln_mix_g[i], ln_mix_b[i])
        xf, xb = moe_ffn_ln(xf, xb, i, moe_w_router[i], moe_router_bias[i], moe_w_gate, moe_w_up, moe_w_down,
                            shared_w_gate[i], shared_w_up[i], shared_w_down[i], ln_ffn_g[i], ln_ffn_b[i])
    return xf.reshape(batch, seq, d)
```

```python
import functools
import math

import jax
import jax.numpy as jnp
from jax import lax
from jax.experimental import pallas as pl
from jax.experimental.pallas import tpu as pltpu

N_HEADS = 16
MOBA_BLOCK = 256
MOBA_TOPK = 3
REL_BUCKETS = 32
REL_MAX_DIST = 2048
CONV_K = 31
TOP_K = 8
N_GROUPS = 8
TOPK_GROUPS = 4
ROUTED_SCALE = 2.5
DEPTH = 4
DN_ALPHA = (2 * DEPTH) ** 0.25
LN_EPS = 1e-5

LANES = 128
SUBLANES = 8
VMEM_LIMIT_BYTES = 56 * 1024 * 1024

NEG_BIG = -1e30
KV_STEP = 2
HEADS_STEP = 4

F32 = jnp.float32
BF16 = jnp.bfloat16


def _params(sem):
    return pltpu.CompilerParams(dimension_semantics=sem, vmem_limit_bytes=VMEM_LIMIT_BYTES)


def _pick(n, pref):
    t = min(pref, n)
    while n % t:
        t //= 2
    return t


def _layer_norm(z, g, b):
    mu = jnp.mean(z, axis=-1, keepdims=True)
    zc = z - mu
    var = jnp.mean(zc * zc, axis=-1, keepdims=True)
    return zc * lax.rsqrt(var + LN_EPS) * g + b


def _dot(a, b):
    return jnp.dot(a, b, preferred_element_type=F32)


def _dot_nt(a, b):
    return lax.dot_general(a, b, (((1,), (1,)), ((), ())), preferred_element_type=F32)


def _mm_kernel(x_ref, w_ref, o_ref):
    o_ref[...] = _dot(x_ref[...], w_ref[...]).astype(o_ref.dtype)


def matmul(x, w, out_dtype, tm=1024, tn=512):
    n, k = x.shape
    m = w.shape[1]
    tm, tn = _pick(n, tm), _pick(m, tn)
    return pl.pallas_call(
        _mm_kernel,
        grid=(n // tm, m // tn),
        in_specs=[pl.BlockSpec((tm, k), lambda i, j: (i, 0)),
                  pl.BlockSpec((k, tn), lambda i, j: (0, j))],
        out_specs=pl.BlockSpec((tm, tn), lambda i, j: (i, j)),
        out_shape=jax.ShapeDtypeStruct((n, m), out_dtype),
        compiler_params=_params(("parallel", "arbitrary")),
        name="matmul",
    )(x, w)


def _glu_kernel(x_ref, wa_ref, wg_ref, ba_ref, bg_ref, o_ref):
    x = x_ref[...]
    a = _dot(x, wa_ref[...]) + ba_ref[...]
    g = _dot(x, wg_ref[...]) + bg_ref[...]
    o_ref[...] = a * jax.nn.sigmoid(g)


def glu_proj(x, w, b, tm=1024, tn=256):
    n, k = x.shape
    d = w.shape[1] // 2
    tm, tn = _pick(n, tm), _pick(d, tn)
    nj = d // tn
    b2 = b.reshape(1, 2 * d)
    return pl.pallas_call(
        _glu_kernel,
        grid=(n // tm, nj),
        in_specs=[pl.BlockSpec((tm, k), lambda i, j: (i, 0)),
                  pl.BlockSpec((k, tn), lambda i, j: (0, j)),
                  pl.BlockSpec((k, tn), lambda i, j: (0, j + nj)),
                  pl.BlockSpec((1, tn), lambda i, j: (0, j)),
                  pl.BlockSpec((1, tn), lambda i, j: (0, j + nj))],
        out_specs=pl.BlockSpec((tm, tn), lambda i, j: (i, j)),
        out_shape=jax.ShapeDtypeStruct((n, d), F32),
        compiler_params=_params(("parallel", "arbitrary")),
        name="glu_proj",
    )(x, w, w, b2, b2)


def _proj_res_ln_kernel(a_ref, w_ref, bias_ref, res_ref, g_ref, b_ref, o_ref, ob_ref):
    y = _dot(a_ref[...], w_ref[...]) + bias_ref[...]
    z = _layer_norm(DN_ALPHA * res_ref[...] + y, g_ref[...], b_ref[...])
    o_ref[...] = z
    ob_ref[...] = z.astype(BF16)


def proj_res_ln(a, w, bias, res, g, b, tm=256):
    n, k = a.shape
    d = w.shape[1]
    tm = _pick(n, tm)
    row = lambda i: (i, 0)
    fix = lambda i: (0, 0)
    return pl.pallas_call(
        _proj_res_ln_kernel,
        grid=(n // tm,),
        in_specs=[pl.BlockSpec((tm, k), row), pl.BlockSpec((k, d), fix),
                  pl.BlockSpec((1, d), fix), pl.BlockSpec((tm, d), row),
                  pl.BlockSpec((1, d), fix), pl.BlockSpec((1, d), fix)],
        out_specs=[pl.BlockSpec((tm, d), row), pl.BlockSpec((tm, d), row)],
        out_shape=[jax.ShapeDtypeStruct((n, d), F32), jax.ShapeDtypeStruct((n, d), BF16)],
        compiler_params=_params(("parallel",)),
        name="proj_res_ln",
    )(a, w, bias.reshape(1, d), res, g.reshape(1, d), b.reshape(1, d))


def _t5_bucket(rel):
    n = jnp.maximum(rel, 0)
    max_exact = REL_BUCKETS // 2
    nf = jnp.maximum(n, 1).astype(F32)
    large = max_exact + (jnp.log(nf / max_exact) / math.log(REL_MAX_DIST / max_exact)
                         * (REL_BUCKETS - max_exact)).astype(jnp.int32)
    large = jnp.minimum(large, REL_BUCKETS - 1)
    return jnp.where(n < max_exact, n, large)


def _bias_tile_kernel(rb_ref, o_ref, *, inv_scale):
    h = pl.program_id(0)
    d = pl.program_id(1)
    blk = o_ref.shape[-1]
    row = lax.broadcasted_iota(jnp.int32, (blk, blk), 0)
    col = lax.broadcasted_iota(jnp.int32, (blk, blk), 1)
    rel = d * blk + row - col
    bucket = _t5_bucket(rel)
    acc = jnp.zeros((blk, blk), F32)
    for k in range(REL_BUCKETS):
        acc = jnp.where(bucket == k, rb_ref[k, h], acc)
    o_ref[0, 0] = jnp.where(rel >= 0, acc * inv_scale, NEG_BIG)


def _num_bias_tiles(nb):
    last_start = (REL_MAX_DIST / (REL_BUCKETS // 2)) ** ((REL_BUCKETS // 2 - 1) / (REL_BUCKETS // 2)) \
        * (REL_BUCKETS // 2)
    d = 1
    while (d - 1) * MOBA_BLOCK + 1 < 1.05 * last_start + 1:
        d += 1
    return min(nb, d + 1)


def bias_tiles(rel_bias, n_heads, n_tiles, scale):
    return pl.pallas_call(
        functools.partial(_bias_tile_kernel, inv_scale=1.0 / scale),
        grid=(n_heads, n_tiles),
        in_specs=[pl.BlockSpec(memory_space=pltpu.SMEM)],
        out_specs=pl.BlockSpec((1, 1, MOBA_BLOCK, MOBA_BLOCK), lambda h, d: (h, d, 0, 0)),
        out_shape=jax.ShapeDtypeStruct((n_heads, n_tiles, MOBA_BLOCK, MOBA_BLOCK), F32),
        compiler_params=_params(("parallel", "parallel")),
        name="bias_tiles",
    )(rel_bias)


def _moba_kernel(q_ref, k_ref, v_ref, bias_ref, o_ref, kmean_sc, kaug_sc, *, nb, n_bias, scale):
    j = pl.program_id(2)
    blk = MOBA_BLOCK
    dh = q_ref.shape[2] // HEADS_STEP
    seq = k_ref.shape[1]
    c_exp = scale * math.log2(math.e)
    cw = KV_STEP * blk

    @pl.when(j == 0)
    def _():
        kmean_sc[...] = jnp.zeros_like(kmean_sc)
        for n in range(nb):
            kn = k_ref[0, n * blk:(n + 1) * blk, :].astype(F32)
            kmean_sc[n:n + 1, :] = jnp.mean(kn, axis=0, keepdims=True)
        blk_of_row = lax.broadcasted_iota(jnp.int32, (seq, LANES), 0) // blk
        blk_lane = lax.broadcasted_iota(jnp.int32, (seq, LANES), 1)
        onehot = jnp.where(blk_of_row == blk_lane, 1.0, 0.0).astype(BF16)
        for h in range(HEADS_STEP):
            kaug_sc[h, :, 0:dh] = k_ref[0, :, h * dh:(h + 1) * dh]
            kaug_sc[h, :, dh:dh + LANES] = onehot

    lane = lax.broadcasted_iota(jnp.int32, (blk, LANES), 1)
    lane_f = lane.astype(F32)
    j0 = pl.multiple_of(j * blk, blk)

    q_aug, state = [], []
    for h in range(HEADS_STEP):
        q = q_ref[0, :, h * dh:(h + 1) * dh]
        gate = _dot_nt(q, kmean_sc[:, h * dh:(h + 1) * dh].astype(BF16))
        gate = jnp.where(lane < j, gate, -jnp.inf)
        sel = jnp.zeros((blk, LANES), F32)
        for _ in range(MOBA_TOPK):
            mx = jnp.max(gate, axis=1, keepdims=True)
            first = jnp.min(jnp.where(gate == mx, lane_f, float(LANES)), axis=1, keepdims=True)
            hit = lane_f == first
            sel = jnp.where(hit, jnp.where(mx > -jnp.inf, 1.0, sel), sel)
            gate = jnp.where(hit, -jnp.inf, gate)
        selneg = jnp.where(sel > 0.0, 0.0, NEG_BIG).astype(BF16)
        q_aug.append(jnp.concatenate([q, selneg], axis=1))

        t = _dot_nt(q, k_ref[0, pl.ds(j0, blk), h * dh:(h + 1) * dh]) + bias_ref[h, 0]
        m0 = jnp.max(t, axis=1, keepdims=True)
        p = jnp.exp2((t - m0) * c_exp)
        l0 = jnp.sum(p, axis=1, keepdims=True)
        acc0 = _dot(p.astype(BF16), v_ref[0, pl.ds(j0, blk), h * dh:(h + 1) * dh])
        state.append((m0, l0, acc0))

    def body(c, carry):
        c0 = pl.multiple_of(c * cw, cw)
        out = []
        for h in range(HEADS_STEP):
            m, l, acc = carry[h]
            bias = jnp.concatenate(
                [bias_ref[h, jnp.clip(j - (c * KV_STEP + u), 0, n_bias - 1)] for u in range(KV_STEP)], axis=1)
            t = _dot_nt(q_aug[h], kaug_sc[h, pl.ds(c0, cw), :]) + bias
            m_new = jnp.maximum(m, jnp.max(t, axis=1, keepdims=True))
            alpha = jnp.exp2((m - m_new) * c_exp)
            p = jnp.exp2((t - m_new) * c_exp)
            l = alpha * l + jnp.sum(p, axis=1, keepdims=True)
            acc = alpha * acc + _dot(p.astype(BF16), v_ref[0, pl.ds(c0, cw), h * dh:(h + 1) * dh])
            out.append((m_new, l, acc))
        return tuple(out)

    state = lax.fori_loop(0, (j + KV_STEP - 1) // KV_STEP, body, tuple(state))
    o_ref[0] = jnp.concatenate([acc / l for (_, l, acc) in state], axis=1).astype(o_ref.dtype)


def moba_attention(qkv, tiles, batch, seq):
    d3 = qkv.shape[1]
    d = d3 // 3
    dh = d // N_HEADS
    nb = seq // MOBA_BLOCK
    assert nb % KV_STEP == 0 and N_HEADS % HEADS_STEP == 0 and dh == LANES
    n_bias = tiles.shape[1]
    hp = N_HEADS // HEADS_STEP
    wh = HEADS_STEP * dh
    qkv3 = qkv.reshape(batch, seq, d3)
    kern = functools.partial(_moba_kernel, nb=nb, n_bias=n_bias, scale=dh ** -0.5)
    out = pl.pallas_call(
        kern,
        grid=(batch, hp, nb),
        in_specs=[pl.BlockSpec((1, MOBA_BLOCK, wh), lambda b, h, j: (b, j, h)),
                  pl.BlockSpec((1, seq, wh), lambda b, h, j: (b, 0, hp + h)),
                  pl.BlockSpec((1, seq, wh), lambda b, h, j: (b, 0, 2 * hp + h)),
                  pl.BlockSpec((HEADS_STEP, n_bias, MOBA_BLOCK, MOBA_BLOCK), lambda b, h, j: (h, 0, 0, 0))],
        out_specs=pl.BlockSpec((1, MOBA_BLOCK, wh), lambda b, h, j: (b, j, h)),
        out_shape=jax.ShapeDtypeStruct((batch, seq, d), BF16),
        scratch_shapes=[pltpu.VMEM((LANES, wh), F32), pltpu.VMEM((HEADS_STEP, seq, dh + LANES), BF16)],
        compiler_params=_params(("parallel", "parallel", "arbitrary")),
        name="moba_attention",
    )(qkv3, qkv3, qkv3, tiles)
    return out.reshape(batch * seq, d)


def _dwconv_ln_kernel(prev_ref, cur_ref, w_ref, bdw_ref, g_ref, b_ref, o_ref, buf, conv_sc, hp_sc, *, halo):
    i = pl.program_id(1)
    ts, d = cur_ref.shape[1], cur_ref.shape[2]
    prev = prev_ref[0]
    buf[0:halo, :] = jnp.where(i > 0, prev, jnp.zeros_like(prev))
    buf[halo:halo + ts, :] = cur_ref[0]
    first = halo - (CONV_K - 1)
    for c in range(d // LANES):
        cs = slice(c * LANES, (c + 1) * LANES)
        acc = jnp.zeros((ts, LANES), F32) + bdw_ref[:, cs]
        for p in range(SUBLANES):
            span = (ts + halo - p) // SUBLANES * SUBLANES
            hp_sc[0:span, :] = buf[p:p + span, cs]
            for a in range(span // SUBLANES):
                k = a * SUBLANES + p - first
                if 0 <= k < CONV_K and a * SUBLANES + ts <= span:
                    acc = acc + hp_sc[a * SUBLANES:a * SUBLANES + ts, :] * w_ref[k:k + 1, cs]
        conv_sc[:, cs] = acc
    y = _layer_norm(conv_sc[...], g_ref[...], b_ref[...])
    o_ref[0] = (y * jax.nn.sigmoid(y)).astype(o_ref.dtype)


def dwconv_ln_silu(h, w_dw, b_dw, g, b, batch, seq, ts=256):
    d = h.shape[1]
    ts = _pick(seq, ts)
    halo = 32
    assert halo >= CONV_K - 1 and ts % halo == 0 and d % LANES == 0
    r = ts // halo
    h3 = h.reshape(batch, seq, d)
    fix = lambda bi, i: (0, 0)
    kern = functools.partial(_dwconv_ln_kernel, halo=halo)
    out = pl.pallas_call(
        kern,
        grid=(batch, seq // ts),
        in_specs=[pl.BlockSpec((1, halo, d), lambda bi, i: (bi, jnp.maximum(i * r - 1, 0), 0)),
                  pl.BlockSpec((1, ts, d), lambda bi, i: (bi, i, 0)),
                  pl.BlockSpec((CONV_K, d), fix), pl.BlockSpec((1, d), fix),
                  pl.BlockSpec((1, d), fix), pl.BlockSpec((1, d), fix)],
        out_specs=pl.BlockSpec((1, ts, d), lambda bi, i: (bi, i, 0)),
        out_shape=jax.ShapeDtypeStruct((batch, seq, d), BF16),
        scratch_shapes=[pltpu.VMEM((halo + ts, d), F32), pltpu.VMEM((ts, d), F32),
                        pltpu.VMEM((halo + ts, LANES), F32)],
        compiler_params=_params(("parallel", "parallel")),
        name="dwconv_ln_silu",
    )(h3, h3, w_dw, b_dw.reshape(1, d), g.reshape(1, d), b.reshape(1, d))
    return out.reshape(batch * seq, d)


def _router_kernel(x_ref, wr_ref, rb_ref, idx_ref, wt_ref, rank_ref, cnt_ref, carry, *, n_exp):
    step = pl.program_id(0)
    tm = x_ref.shape[0]
    per_g = n_exp // N_GROUPS
    lane = lax.broadcasted_iota(jnp.int32, (tm, LANES), 1)
    lane_f = lane.astype(F32)
    grp_f = (lane // per_g).astype(F32)
    valid = lane < n_exp
    logits = _dot(x_ref[...], wr_ref[...])
    scores = jax.nn.sigmoid(logits)
    biased = jnp.where(valid, scores + rb_ref[...], -jnp.inf)

    def first_argmax(v):
        mx = jnp.max(v, axis=1, keepdims=True)
        first = jnp.min(jnp.where(v == mx, lane_f, float(LANES)), axis=1, keepdims=True)
        return mx, first

    gscore = jnp.full((tm, LANES), -jnp.inf, F32)
    for g in range(N_GROUPS):
        vg = jnp.where(grp_f == float(g), biased, -jnp.inf)
        m1, f1 = first_argmax(vg)
        m2 = jnp.max(jnp.where(lane_f == f1, -jnp.inf, vg), axis=1, keepdims=True)
        gscore = jnp.where(lane == g, m1 + m2, gscore)
    keep = jnp.zeros((tm, LANES), F32)
    for _ in range(TOPK_GROUPS):
        _, fg = first_argmax(gscore)
        keep = jnp.where(grp_f == fg, 1.0, keep)
        gscore = jnp.where(lane_f == fg, -jnp.inf, gscore)
    cand = jnp.where(keep > 0.0, biased, -jnp.inf)
    idx = jnp.zeros((tm, LANES), F32)
    wts = jnp.zeros((tm, LANES), F32)
    chosen = jnp.zeros((tm, LANES), F32)
    picks = []
    for k in range(TOP_K):
        _, fe = first_argmax(cand)
        hit = lane_f == fe
        wk = jnp.sum(jnp.where(hit, scores, 0.0), axis=1, keepdims=True)
        idx = jnp.where(lane == k, fe, idx)
        wts = jnp.where(lane == k, wk, wts)
        chosen = jnp.where(hit, 1.0, chosen)
        cand = jnp.where(hit, -jnp.inf, cand)
        picks.append(fe)
    wsum = jnp.sum(wts, axis=1, keepdims=True)
    idx_ref[...] = idx.astype(jnp.int32)
    wt_ref[...] = wts / wsum * ROUTED_SCALE

    @pl.when(step == 0)
    def _():
        carry[...] = jnp.zeros_like(carry)

    r_i = lax.broadcasted_iota(jnp.int32, (tm, tm), 0)
    c_i = lax.broadcasted_iota(jnp.int32, (tm, tm), 1)
    earlier = jnp.where(r_i > c_i, 1.0, 0.0).astype(BF16)
    rank = _dot(earlier, chosen.astype(BF16)) + carry[...]
    rank_sel = jnp.zeros((tm, LANES), F32)
    for k in range(TOP_K):
        rk = jnp.sum(jnp.where(lane_f == picks[k], rank, 0.0), axis=1, keepdims=True)
        rank_sel = jnp.where(lane == k, rk, rank_sel)
    rank_ref[...] = rank_sel.astype(jnp.int32)
    total = carry[...] + jnp.sum(chosen, axis=0, keepdims=True)
    carry[...] = total
    cnt_ref[...] = total.astype(jnp.int32)


def router(x, w_router, r_bias, tm=256):
    n, d = x.shape
    n_exp = w_router.shape[1]
    assert n_exp <= LANES and n_exp % N_GROUPS == 0
    tm = _pick(n, tm)
    wr = jnp.pad(w_router, ((0, 0), (0, LANES - n_exp))).astype(BF16)
    rb = jnp.pad(r_bias, (0, LANES - n_exp)).reshape(1, LANES)
    row = lambda i: (i, 0)
    fix = lambda i: (0, 0)
    return pl.pallas_call(
        functools.partial(_router_kernel, n_exp=n_exp),
        grid=(n // tm,),
        in_specs=[pl.BlockSpec((tm, d), row), pl.BlockSpec((d, LANES), fix), pl.BlockSpec((1, LANES), fix)],
        out_specs=[pl.BlockSpec((tm, LANES), row), pl.BlockSpec((tm, LANES), row),
                   pl.BlockSpec((tm, LANES), row), pl.BlockSpec((1, LANES), fix)],
        out_shape=[jax.ShapeDtypeStruct((n, LANES), jnp.int32), jax.ShapeDtypeStruct((n, LANES), F32),
                   jax.ShapeDtypeStruct((n, LANES), jnp.int32), jax.ShapeDtypeStruct((1, LANES), jnp.int32)],
        scratch_shapes=[pltpu.VMEM((1, LANES), F32)],
        compiler_params=_params(("arbitrary",)),
        name="router",
    )(x, wr, rb)


def expert_plan(idx, rank, counts, n_exp, tm):
    n_tok = idx.shape[0]
    n_asg = n_tok * TOP_K
    n_tiles = n_asg // tm
    n_items = n_tiles + n_exp - 1
    counts = counts[0, :n_exp]
    ends = jnp.cumsum(counts)
    starts = ends - counts
    first_blk = starts // tm
    n_items_e = jnp.where(counts > 0, (ends - 1) // tm - first_blk + 1, 0)
    item_end = jnp.cumsum(n_items_e)
    item_start = item_end - n_items_e
    n_live = item_end[-1]

    e_ids = jnp.arange(n_exp, dtype=jnp.int32)
    mine = idx[:, :TOP_K, None] == e_ids
    lookup = lambda table: jnp.sum(jnp.where(mine, table, 0), axis=-1)
    pos = rank[:, :TOP_K] + lookup(starts)
    pos_out = (lookup(item_start - first_blk) + pos // tm) * tm + pos % tm

    it = jnp.minimum(jnp.arange(n_items, dtype=jnp.int32), n_live - 1)
    onehot = (jnp.sum(item_end[None, :] <= it[:, None], axis=1)[:, None] == e_ids).astype(jnp.int32)
    pick = lambda v: jnp.sum(onehot * v[None, :], axis=1)
    item_exp = pick(e_ids)
    item_blk = jnp.clip(pick(first_blk) + it - pick(item_start), 0, n_tiles - 1)
    new_exp = jnp.concatenate([jnp.ones((1,), jnp.int32), (item_exp[1:] != item_exp[:-1]).astype(jnp.int32)])
    tables = tuple(t.astype(jnp.int32) for t in (item_blk, item_exp, new_exp, n_live.reshape(1)))
    return pos.astype(jnp.int32), pos_out.astype(jnp.int32), tables


def _dispatch_kernel(pos_ref, x_ref, xs_hbm, sem):
    tt = x_ref.shape[0]
    for t in range(tt):
        for k in range(TOP_K):
            pltpu.make_async_copy(x_ref.at[pl.ds(t, 1)], xs_hbm.at[pl.ds(pos_ref[0, 0, t * TOP_K + k], 1)],
                                  sem).start(priority=k % 2)
    for _ in range(TOP_K):
        pltpu.make_async_copy(x_ref, xs_hbm.at[pl.ds(0, tt)], sem).wait()


def dispatch(x, pos, tt=128):
    n, d = x.shape
    tt = _pick(n, tt)
    pos3 = pos.reshape(n // tt, 1, tt * TOP_K)
    return pl.pallas_call(
        _dispatch_kernel,
        grid=(n // tt,),
        in_specs=[pl.BlockSpec((1, 1, tt * TOP_K), lambda i: (i, 0, 0), memory_space=pltpu.SMEM),
                  pl.BlockSpec((tt, d), lambda i: (i, 0))],
        out_specs=pl.BlockSpec(memory_space=pl.ANY),
        out_shape=jax.ShapeDtypeStruct((n * TOP_K, d), x.dtype),
        scratch_shapes=[pltpu.SemaphoreType.DMA],
        compiler_params=_params(("arbitrary",)),
        name="dispatch",
    )(pos3, x)


def _experts_kernel(blk_ref, exp_ref, newexp_ref, nlive_ref, xs_ref, wg_ref, wu_ref, wd_ref, y_ref, wgb, wub, wdb):
    i = pl.program_id(0)

    @pl.when(i < nlive_ref[0])
    def _():
        @pl.when(newexp_ref[i] == 1)
        def _():
            wgb[...] = wg_ref[...].astype(BF16)
            wub[...] = wu_ref[...].astype(BF16)
            wdb[...] = wd_ref[...].astype(BF16)

        x = xs_ref[...].astype(BF16)
        g = _dot(x, wgb[...])
        u = _dot(x, wub[...])
        h = (g * jax.nn.sigmoid(g) * u).astype(BF16)
        y_ref[...] = _dot(h, wdb[...])

    @pl.when(i >= nlive_ref[0])
    def _():
        y_ref[...] = jnp.zeros_like(y_ref)


def routed_experts(xs, tables, w_gate, w_up, w_down, layer, tm):
    n_asg, d = xs.shape
    n_exp, f = w_gate.shape[1], w_gate.shape[3]
    n_items = n_asg // tm + n_exp - 1
    w_in = pl.BlockSpec((None, None, d, f), lambda i, blk, ex, *_: (layer, ex[i], 0, 0))
    w_out = pl.BlockSpec((None, None, f, d), lambda i, blk, ex, *_: (layer, ex[i], 0, 0))
    grid_spec = pltpu.PrefetchScalarGridSpec(
        num_scalar_prefetch=4,
        grid=(n_items,),
        in_specs=[pl.BlockSpec((tm, d), lambda i, blk, *_: (blk[i], 0)), w_in, w_in, w_out],
        out_specs=pl.BlockSpec((tm, d), lambda i, *_: (i, 0)),
        scratch_shapes=[pltpu.VMEM((d, f), BF16), pltpu.VMEM((d, f), BF16), pltpu.VMEM((f, d), BF16)],
    )
    return pl.pallas_call(
        _experts_kernel,
        grid_spec=grid_spec,
        out_shape=jax.ShapeDtypeStruct((n_items * tm, d), F32),
        compiler_params=_params(("arbitrary",)),
        name="routed_experts",
    )(*tables, xs, w_gate, w_up, w_down)


def _combine_kernel(pos_ref, posn_ref, y_hbm, wt_ref, xb_ref, sg_ref, su_ref, sd_ref, res_ref, g_ref, b_ref,
                    o_ref, ob_ref, ybuf, sems):
    i = pl.program_id(0)
    n_steps = pl.num_programs(0)
    tc = res_ref.shape[0]

    def gather_wait(s):
        for k in range(TOP_K):
            pltpu.make_async_copy(y_hbm.at[pl.ds(0, tc)], ybuf.at[s, k], sems.at[s]).wait()

    @pl.when(i == 0)
    def _():
        def token(t, c):
            for k in range(TOP_K):
                pltpu.make_async_copy(y_hbm.at[pl.ds(pos_ref[0, 0, t * TOP_K + k], 1)],
                                      ybuf.at[0, k, pl.ds(t, 1)], sems.at[0]).start(priority=k % 2)
            return c
        lax.fori_loop(0, tc, token, 0)

    def step(slot):
        nxt = 1 - slot
        for t in range(tc):
            for k in range(TOP_K):
                pltpu.make_async_copy(y_hbm.at[pl.ds(posn_ref[0, 0, t * TOP_K + k], 1)],
                                      ybuf.at[nxt, k, pl.ds(t, 1)], sems.at[nxt]).start(priority=k % 2)

        xb = xb_ref[...]
        hg = _dot(xb, sg_ref[...])
        hu = _dot(xb, su_ref[...])
        shared = _dot((hg * jax.nn.sigmoid(hg) * hu).astype(BF16), sd_ref[...])
        base = DN_ALPHA * res_ref[...] + shared

        gather_wait(slot)
        wts = wt_ref[...]
        routed = ybuf[slot, 0] * wts[:, 0:1]
        for k in range(1, TOP_K):
            routed = routed + ybuf[slot, k] * wts[:, k:k + 1]
        z = _layer_norm(base + routed, g_ref[...], b_ref[...])
        o_ref[...] = z
        ob_ref[...] = z.astype(BF16)

        @pl.when(i == n_steps - 1)
        def _():
            gather_wait(nxt)

    for slot in range(2):
        pl.when(i % 2 == slot)(functools.partial(step, slot))


def combine_shared_ln(y, pos, wts, xb, s_gate, s_up, s_down, res, g, b, tc=128):
    n, d = res.shape
    f = s_gate.shape[1]
    tc = _pick(n, tc)
    n_steps = n // tc
    pos3 = pos.reshape(n_steps, 1, tc * TOP_K)
    row = lambda i: (i, 0)
    fix = lambda i: (0, 0)
    pos_blk = lambda imap: pl.BlockSpec((1, 1, tc * TOP_K), imap, memory_space=pltpu.SMEM)
    return pl.pallas_call(
        _combine_kernel,
        grid=(n_steps,),
        in_specs=[pos_blk(lambda i: (i, 0, 0)),
                  pos_blk(lambda i: (jnp.minimum(i + 1, n_steps - 1), 0, 0)),
                  pl.BlockSpec(memory_space=pl.ANY),
                  pl.BlockSpec((tc, LANES), row), pl.BlockSpec((tc, d), row),
                  pl.BlockSpec((d, f), fix), pl.BlockSpec((d, f), fix), pl.BlockSpec((f, d), fix),
                  pl.BlockSpec((tc, d), row), pl.BlockSpec((1, d), fix), pl.BlockSpec((1, d), fix)],
        out_specs=[pl.BlockSpec((tc, d), row), pl.BlockSpec((tc, d), row)],
        out_shape=[jax.ShapeDtypeStruct((n, d), F32), jax.ShapeDtypeStruct((n, d), BF16)],
        scratch_shapes=[pltpu.VMEM((2, TOP_K, tc, d), F32), pltpu.SemaphoreType.DMA((2,))],
        compiler_params=_params(("arbitrary",)),
        name="combine_shared_ln",
    )(pos3, pos3, y, wts, xb, s_gate, s_up, s_down, res, g.reshape(1, d), b.reshape(1, d))


def moe_ffn_ln(xf, xb, layer, w_router, r_bias, w_gate, w_up, w_down, s_gate, s_up, s_down, g, b, tm=256):
    n_exp = w_router.shape[1]
    tm = _pick(xf.shape[0] * TOP_K, tm)
    idx, wts, rank, counts = router(xb, w_router, r_bias)
    pos, pos_out, tables = expert_plan(idx, rank, counts, n_exp, tm)
    xs = dispatch(xf, pos)
    y = routed_experts(xs, tables, w_gate, w_up, w_down, layer, tm)
    return combine_shared_ln(y, pos_out, wts, xb, s_gate.astype(BF16), s_up.astype(BF16), s_down.astype(BF16),
                             xf, g, b)


def kernel(x, rel_bias, attn_w_qkv, attn_w_o, conv_w_in, conv_b_in, conv_w_dw, conv_b_dw,
           conv_ln_g, conv_ln_b, conv_w_out, conv_b_out, ln_mix_g, ln_mix_b, ln_ffn_g, ln_ffn_b,
           moe_w_router, moe_router_bias, moe_w_gate, moe_w_up, moe_w_down,
           shared_w_gate, shared_w_up, shared_w_down):
    batch, seq, d = x.shape
    n = batch * seq
    depth = ln_mix_g.shape[0]
    nb = seq // MOBA_BLOCK
    assert seq % MOBA_BLOCK == 0 and d % N_HEADS == 0 and depth == DEPTH

    tiles = bias_tiles(rel_bias, N_HEADS, _num_bias_tiles(nb), (d // N_HEADS) ** -0.5)
    xf = x.reshape(n, d)
    xb = xf.astype(BF16)
    zero_bias = jnp.zeros((d,), F32)
    for i in range(depth):
        m = i // 2
        if i % 2 == 0:
            qkv = matmul(xb, attn_w_qkv[m].astype(BF16), BF16)
            a = moba_attention(qkv, tiles, batch, seq)
            xf, xb = proj_res_ln(a, attn_w_o[m].astype(BF16), zero_bias, xf, ln_mix_g[i], ln_mix_b[i])
        else:
            h = glu_proj(xb, conv_w_in[m].astype(BF16), conv_b_in[m])
            a = dwconv_ln_silu(h, conv_w_dw[m], conv_b_dw[m], conv_ln_g[m], conv_ln_b[m], batch, seq)
            xf, xb = proj_res_ln(a, conv_w_out[m].astype(BF16), conv_b_out[m], xf, ln_mix_g[i], ln_mix_b[i])
        xf, xb = moe_ffn_ln(xf, xb, i, moe_w_router[i], moe_router_bias[i], moe_w_gate, moe_w_up, moe_w_down,
                            shared_w_gate[i], shared_w_up[i], shared_w_down[i], ln_ffn_g[i], ln_ffn_b[i])
    return xf.reshape(batch, seq, d)
```

```python
import functools
import math

import jax
import jax.numpy as jnp
from jax import lax
from jax.experimental import pallas as pl
from jax.experimental.pallas import tpu as pltpu

N_HEADS = 16
MOBA_BLOCK = 256
MOBA_TOPK = 3
REL_BUCKETS = 32
REL_MAX_DIST = 2048
CONV_K = 31
TOP_K = 8
N_GROUPS = 8
TOPK_GROUPS = 4
ROUTED_SCALE = 2.5
DEPTH = 4
DN_ALPHA = (2 * DEPTH) ** 0.25
LN_EPS = 1e-5

LANES = 128
SUBLANES = 8
VMEM_LIMIT_BYTES = 56 * 1024 * 1024

NEG_BIG = -1e30
KV_STEP = 2
HEADS_STEP = 4

F32 = jnp.float32
BF16 = jnp.bfloat16


def _params(sem):
    return pltpu.CompilerParams(dimension_semantics=sem, vmem_limit_bytes=VMEM_LIMIT_BYTES)


def _pick(n, pref):
    t = min(pref, n)
    while n % t:
        t //= 2
    return t


def _layer_norm(z, g, b):
    mu = jnp.mean(z, axis=-1, keepdims=True)
    zc = z - mu
    var = jnp.mean(zc * zc, axis=-1, keepdims=True)
    return zc * lax.rsqrt(var + LN_EPS) * g + b


def _dot(a, b):
    return jnp.dot(a, b, preferred_element_type=F32)


def _dot_nt(a, b):
    return lax.dot_general(a, b, (((1,), (1,)), ((), ())), preferred_element_type=F32)


U32 = jnp.uint32
HI_MASK = 0xFFFF0000


def _pack_halves(lo, hi):
    lo_bits = pltpu.bitcast(lo.astype(BF16).astype(F32), U32) >> 16
    hi_bits = pltpu.bitcast(hi.astype(BF16).astype(F32), U32) & U32(HI_MASK)
    return lo_bits | hi_bits


def _unpack_halves(w):
    return pltpu.bitcast(w << 16, F32), pltpu.bitcast(w & U32(HI_MASK), F32)


def pack_rows(xb):
    half = xb.shape[1] // 2
    bits = lax.bitcast_convert_type(xb, jnp.uint16).astype(U32)
    return bits[:, :half] | (bits[:, half:] << 16)


def _mm_kernel(x_ref, w_ref, o_ref):
    o_ref[...] = _dot(x_ref[...], w_ref[...]).astype(o_ref.dtype)


def matmul(x, w, out_dtype, tm=1024, tn=512):
    n, k = x.shape
    m = w.shape[1]
    tm, tn = _pick(n, tm), _pick(m, tn)
    return pl.pallas_call(
        _mm_kernel,
        grid=(n // tm, m // tn),
        in_specs=[pl.BlockSpec((tm, k), lambda i, j: (i, 0)),
                  pl.BlockSpec((k, tn), lambda i, j: (0, j))],
        out_specs=pl.BlockSpec((tm, tn), lambda i, j: (i, j)),
        out_shape=jax.ShapeDtypeStruct((n, m), out_dtype),
        compiler_params=_params(("parallel", "arbitrary")),
        name="matmul",
    )(x, w)


def _glu_kernel(x_ref, wa_ref, wg_ref, ba_ref, bg_ref, o_ref):
    x = x_ref[...]
    a = _dot(x, wa_ref[...]) + ba_ref[...]
    g = _dot(x, wg_ref[...]) + bg_ref[...]
    o_ref[...] = a * jax.nn.sigmoid(g)


def glu_proj(x, w, b, tm=1024, tn=256):
    n, k = x.shape
    d = w.shape[1] // 2
    tm, tn = _pick(n, tm), _pick(d, tn)
    nj = d // tn
    b2 = b.reshape(1, 2 * d)
    return pl.pallas_call(
        _glu_kernel,
        grid=(n // tm, nj),
        in_specs=[pl.BlockSpec((tm, k), lambda i, j: (i, 0)),
                  pl.BlockSpec((k, tn), lambda i, j: (0, j)),
                  pl.BlockSpec((k, tn), lambda i, j: (0, j + nj)),
                  pl.BlockSpec((1, tn), lambda i, j: (0, j)),
                  pl.BlockSpec((1, tn), lambda i, j: (0, j + nj))],
        out_specs=pl.BlockSpec((tm, tn), lambda i, j: (i, j)),
        out_shape=jax.ShapeDtypeStruct((n, d), F32),
        compiler_params=_params(("parallel", "arbitrary")),
        name="glu_proj",
    )(x, w, w, b2, b2)


def _proj_res_ln_kernel(a_ref, w_ref, bias_ref, res_ref, g_ref, b_ref, o_ref, ob_ref):
    y = _dot(a_ref[...], w_ref[...]) + bias_ref[...]
    z = _layer_norm(DN_ALPHA * res_ref[...] + y, g_ref[...], b_ref[...])
    o_ref[...] = z
    ob_ref[...] = z.astype(BF16)


def proj_res_ln(a, w, bias, res, g, b, tm=256):
    n, k = a.shape
    d = w.shape[1]
    tm = _pick(n, tm)
    row = lambda i: (i, 0)
    fix = lambda i: (0, 0)
    return pl.pallas_call(
        _proj_res_ln_kernel,
        grid=(n // tm,),
        in_specs=[pl.BlockSpec((tm, k), row), pl.BlockSpec((k, d), fix),
                  pl.BlockSpec((1, d), fix), pl.BlockSpec((tm, d), row),
                  pl.BlockSpec((1, d), fix), pl.BlockSpec((1, d), fix)],
        out_specs=[pl.BlockSpec((tm, d), row), pl.BlockSpec((tm, d), row)],
        out_shape=[jax.ShapeDtypeStruct((n, d), F32), jax.ShapeDtypeStruct((n, d), BF16)],
        compiler_params=_params(("parallel",)),
        name="proj_res_ln",
    )(a, w, bias.reshape(1, d), res, g.reshape(1, d), b.reshape(1, d))


def _t5_bucket(rel):
    n = jnp.maximum(rel, 0)
    max_exact = REL_BUCKETS // 2
    nf = jnp.maximum(n, 1).astype(F32)
    large = max_exact + (jnp.log(nf / max_exact) / math.log(REL_MAX_DIST / max_exact)
                         * (REL_BUCKETS - max_exact)).astype(jnp.int32)
    large = jnp.minimum(large, REL_BUCKETS - 1)
    return jnp.where(n < max_exact, n, large)


def _bias_tile_kernel(rb_ref, o_ref, *, inv_scale):
    h = pl.program_id(0)
    d = pl.program_id(1)
    blk = o_ref.shape[-1]
    row = lax.broadcasted_iota(jnp.int32, (blk, blk), 0)
    col = lax.broadcasted_iota(jnp.int32, (blk, blk), 1)
    rel = d * blk + row - col
    bucket = _t5_bucket(rel)
    acc = jnp.zeros((blk, blk), F32)
    for k in range(REL_BUCKETS):
        acc = jnp.where(bucket == k, rb_ref[k, h], acc)
    o_ref[0, 0] = jnp.where(rel >= 0, acc * inv_scale, NEG_BIG)


def _num_bias_tiles(nb):
    last_start = (REL_MAX_DIST / (REL_BUCKETS // 2)) ** ((REL_BUCKETS // 2 - 1) / (REL_BUCKETS // 2)) \
        * (REL_BUCKETS // 2)
    d = 1
    while (d - 1) * MOBA_BLOCK + 1 < 1.05 * last_start + 1:
        d += 1
    return min(nb, d + 1)


def bias_tiles(rel_bias, n_heads, n_tiles, scale):
    return pl.pallas_call(
        functools.partial(_bias_tile_kernel, inv_scale=1.0 / scale),
        grid=(n_heads, n_tiles),
        in_specs=[pl.BlockSpec(memory_space=pltpu.SMEM)],
        out_specs=pl.BlockSpec((1, 1, MOBA_BLOCK, MOBA_BLOCK), lambda h, d: (h, d, 0, 0)),
        out_shape=jax.ShapeDtypeStruct((n_heads, n_tiles, MOBA_BLOCK, MOBA_BLOCK), F32),
        compiler_params=_params(("parallel", "parallel")),
        name="bias_tiles",
    )(rel_bias)


def _moba_kernel(q_ref, k_ref, v_ref, bias_ref, o_ref, kmean_sc, kaug_sc, *, nb, n_bias, scale):
    j = pl.program_id(2)
    blk = MOBA_BLOCK
    dh = q_ref.shape[2] // HEADS_STEP
    seq = k_ref.shape[1]
    c_exp = scale * math.log2(math.e)
    cw = KV_STEP * blk

    @pl.when(j == 0)
    def _():
        kmean_sc[...] = jnp.zeros_like(kmean_sc)
        for n in range(nb):
            kn = k_ref[0, n * blk:(n + 1) * blk, :].astype(F32)
            kmean_sc[n:n + 1, :] = jnp.mean(kn, axis=0, keepdims=True)
        blk_of_row = lax.broadcasted_iota(jnp.int32, (seq, LANES), 0) // blk
        blk_lane = lax.broadcasted_iota(jnp.int32, (seq, LANES), 1)
        onehot = jnp.where(blk_of_row == blk_lane, 1.0, 0.0).astype(BF16)
        for h in range(HEADS_STEP):
            kaug_sc[h, :, 0:dh] = k_ref[0, :, h * dh:(h + 1) * dh]
            kaug_sc[h, :, dh:dh + LANES] = onehot

    lane = lax.broadcasted_iota(jnp.int32, (blk, LANES), 1)
    lane_f = lane.astype(F32)
    j0 = pl.multiple_of(j * blk, blk)

    q_aug, state = [], []
    for h in range(HEADS_STEP):
        q = q_ref[0, :, h * dh:(h + 1) * dh]
        gate = _dot_nt(q, kmean_sc[:, h * dh:(h + 1) * dh].astype(BF16))
        gate = jnp.where(lane < j, gate, -jnp.inf)
        sel = jnp.zeros((blk, LANES), F32)
        for _ in range(MOBA_TOPK):
            mx = jnp.max(gate, axis=1, keepdims=True)
            first = jnp.min(jnp.where(gate == mx, lane_f, float(LANES)), axis=1, keepdims=True)
            hit = lane_f == first
            sel = jnp.where(hit, jnp.where(mx > -jnp.inf, 1.0, sel), sel)
            gate = jnp.where(hit, -jnp.inf, gate)
        selneg = jnp.where(sel > 0.0, 0.0, NEG_BIG).astype(BF16)
        q_aug.append(jnp.concatenate([q, selneg], axis=1))

        t = _dot_nt(q, k_ref[0, pl.ds(j0, blk), h * dh:(h + 1) * dh]) + bias_ref[h, 0]
        m0 = jnp.max(t, axis=1, keepdims=True)
        p = jnp.exp2((t - m0) * c_exp)
        l0 = jnp.sum(p, axis=1, keepdims=True)
        acc0 = _dot(p.astype(BF16), v_ref[0, pl.ds(j0, blk), h * dh:(h + 1) * dh])
        state.append((m0, l0, acc0))

    def body(c, carry):
        c0 = pl.multiple_of(c * cw, cw)
        out = []
        for h in range(HEADS_STEP):
            m, l, acc = carry[h]
            bias = jnp.concatenate(
                [bias_ref[h, jnp.clip(j - (c * KV_STEP + u), 0, n_bias - 1)] for u in range(KV_STEP)], axis=1)
            t = _dot_nt(q_aug[h], kaug_sc[h, pl.ds(c0, cw), :]) + bias
            m_new = jnp.maximum(m, jnp.max(t, axis=1, keepdims=True))
            alpha = jnp.exp2((m - m_new) * c_exp)
            p = jnp.exp2((t - m_new) * c_exp)
            l = alpha * l + jnp.sum(p, axis=1, keepdims=True)
            acc = alpha * acc + _dot(p.astype(BF16), v_ref[0, pl.ds(c0, cw), h * dh:(h + 1) * dh])
            out.append((m_new, l, acc))
        return tuple(out)

    state = lax.fori_loop(0, (j + KV_STEP - 1) // KV_STEP, body, tuple(state))
    o_ref[0] = jnp.concatenate([acc / l for (_, l, acc) in state], axis=1).astype(o_ref.dtype)


def moba_attention(qkv, tiles, batch, seq):
    d3 = qkv.shape[1]
    d = d3 // 3
    dh = d // N_HEADS
    nb = seq // MOBA_BLOCK
    assert nb % KV_STEP == 0 and N_HEADS % HEADS_STEP == 0 and dh == LANES
    n_bias = tiles.shape[1]
    hp = N_HEADS // HEADS_STEP
    wh = HEADS_STEP * dh
    qkv3 = qkv.reshape(batch, seq, d3)
    kern = functools.partial(_moba_kernel, nb=nb, n_bias=n_bias, scale=dh ** -0.5)
    out = pl.pallas_call(
        kern,
        grid=(batch, hp, nb),
        in_specs=[pl.BlockSpec((1, MOBA_BLOCK, wh), lambda b, h, j: (b, j, h)),
                  pl.BlockSpec((1, seq, wh), lambda b, h, j: (b, 0, hp + h)),
                  pl.BlockSpec((1, seq, wh), lambda b, h, j: (b, 0, 2 * hp + h)),
                  pl.BlockSpec((HEADS_STEP, n_bias, MOBA_BLOCK, MOBA_BLOCK), lambda b, h, j: (h, 0, 0, 0))],
        out_specs=pl.BlockSpec((1, MOBA_BLOCK, wh), lambda b, h, j: (b, j, h)),
        out_shape=jax.ShapeDtypeStruct((batch, seq, d), BF16),
        scratch_shapes=[pltpu.VMEM((LANES, wh), F32), pltpu.VMEM((HEADS_STEP, seq, dh + LANES), BF16)],
        compiler_params=_params(("parallel", "parallel", "arbitrary")),
        name="moba_attention",
    )(qkv3, qkv3, qkv3, tiles)
    return out.reshape(batch * seq, d)


def _dwconv_ln_kernel(prev_ref, cur_ref, w_ref, bdw_ref, g_ref, b_ref, o_ref, buf, conv_sc, hp_sc, *, halo):
    i = pl.program_id(1)
    ts, d = cur_ref.shape[1], cur_ref.shape[2]
    prev = prev_ref[0]
    buf[0:halo, :] = jnp.where(i > 0, prev, jnp.zeros_like(prev))
    buf[halo:halo + ts, :] = cur_ref[0]
    first = halo - (CONV_K - 1)
    for c in range(d // LANES):
        cs = slice(c * LANES, (c + 1) * LANES)
        acc = jnp.zeros((ts, LANES), F32) + bdw_ref[:, cs]
        for p in range(SUBLANES):
            span = (ts + halo - p) // SUBLANES * SUBLANES
            hp_sc[0:span, :] = buf[p:p + span, cs]
            for a in range(span // SUBLANES):
                k = a * SUBLANES + p - first
                if 0 <= k < CONV_K and a * SUBLANES + ts <= span:
                    acc = acc + hp_sc[a * SUBLANES:a * SUBLANES + ts, :] * w_ref[k:k + 1, cs]
        conv_sc[:, cs] = acc
    y = _layer_norm(conv_sc[...], g_ref[...], b_ref[...])
    o_ref[0] = (y * jax.nn.sigmoid(y)).astype(o_ref.dtype)


def dwconv_ln_silu(h, w_dw, b_dw, g, b, batch, seq, ts=256):
    d = h.shape[1]
    ts = _pick(seq, ts)
    halo = 32
    assert halo >= CONV_K - 1 and ts % halo == 0 and d % LANES == 0
    r = ts // halo
    h3 = h.reshape(batch, seq, d)
    fix = lambda bi, i: (0, 0)
    kern = functools.partial(_dwconv_ln_kernel, halo=halo)
    out = pl.pallas_call(
        kern,
        grid=(batch, seq // ts),
        in_specs=[pl.BlockSpec((1, halo, d), lambda bi, i: (bi, jnp.maximum(i * r - 1, 0), 0)),
                  pl.BlockSpec((1, ts, d), lambda bi, i: (bi, i, 0)),
                  pl.BlockSpec((CONV_K, d), fix), pl.BlockSpec((1, d), fix),
                  pl.BlockSpec((1, d), fix), pl.BlockSpec((1, d), fix)],
        out_specs=pl.BlockSpec((1, ts, d), lambda bi, i: (bi, i, 0)),
        out_shape=jax.ShapeDtypeStruct((batch, seq, d), BF16),
        scratch_shapes=[pltpu.VMEM((halo + ts, d), F32), pltpu.VMEM((ts, d), F32),
                        pltpu.VMEM((halo + ts, LANES), F32)],
        compiler_params=_params(("parallel", "parallel")),
        name="dwconv_ln_silu",
    )(h3, h3, w_dw, b_dw.reshape(1, d), g.reshape(1, d), b.reshape(1, d))
    return out.reshape(batch * seq, d)


def _router_kernel(x_ref, wr_ref, rb_ref, idx_ref, wt_ref, rank_ref, cnt_ref, carry, *, n_exp):
    step = pl.program_id(0)
    tm = x_ref.shape[0]
    per_g = n_exp // N_GROUPS
    lane = lax.broadcasted_iota(jnp.int32, (tm, LANES), 1)
    lane_f = lane.astype(F32)
    grp_f = (lane // per_g).astype(F32)
    valid = lane < n_exp
    logits = _dot(x_ref[...], wr_ref[...])
    scores = jax.nn.sigmoid(logits)
    biased = jnp.where(valid, scores + rb_ref[...], -jnp.inf)

    def first_argmax(v):
        mx = jnp.max(v, axis=1, keepdims=True)
        first = jnp.min(jnp.where(v == mx, lane_f, float(LANES)), axis=1, keepdims=True)
        return mx, first

    gscore = jnp.full((tm, LANES), -jnp.inf, F32)
    for g in range(N_GROUPS):
        vg = jnp.where(grp_f == float(g), biased, -jnp.inf)
        m1, f1 = first_argmax(vg)
        m2 = jnp.max(jnp.where(lane_f == f1, -jnp.inf, vg), axis=1, keepdims=True)
        gscore = jnp.where(lane == g, m1 + m2, gscore)
    keep = jnp.zeros((tm, LANES), F32)
    for _ in range(TOPK_GROUPS):
        _, fg = first_argmax(gscore)
        keep = jnp.where(grp_f == fg, 1.0, keep)
        gscore = jnp.where(lane_f == fg, -jnp.inf, gscore)
    cand = jnp.where(keep > 0.0, biased, -jnp.inf)
    idx = jnp.zeros((tm, LANES), F32)
    wts = jnp.zeros((tm, LANES), F32)
    chosen = jnp.zeros((tm, LANES), F32)
    picks = []
    for k in range(TOP_K):
        _, fe = first_argmax(cand)
        hit = lane_f == fe
        wk = jnp.sum(jnp.where(hit, scores, 0.0), axis=1, keepdims=True)
        idx = jnp.where(lane == k, fe, idx)
        wts = jnp.where(lane == k, wk, wts)
        chosen = jnp.where(hit, 1.0, chosen)
        cand = jnp.where(hit, -jnp.inf, cand)
        picks.append(fe)
    wsum = jnp.sum(wts, axis=1, keepdims=True)
    idx_ref[...] = idx.astype(jnp.int32)
    wt_ref[...] = wts / wsum * ROUTED_SCALE

    @pl.when(step == 0)
    def _():
        carry[...] = jnp.zeros_like(carry)

    r_i = lax.broadcasted_iota(jnp.int32, (tm, tm), 0)
    c_i = lax.broadcasted_iota(jnp.int32, (tm, tm), 1)
    earlier = jnp.where(r_i > c_i, 1.0, 0.0).astype(BF16)
    rank = _dot(earlier, chosen.astype(BF16)) + carry[...]
    rank_sel = jnp.zeros((tm, LANES), F32)
    for k in range(TOP_K):
        rk = jnp.sum(jnp.where(lane_f == picks[k], rank, 0.0), axis=1, keepdims=True)
        rank_sel = jnp.where(lane == k, rk, rank_sel)
    rank_ref[...] = rank_sel.astype(jnp.int32)
    total = carry[...] + jnp.sum(chosen, axis=0, keepdims=True)
    carry[...] = total
    cnt_ref[...] = total.astype(jnp.int32)


def router(x, w_router, r_bias, tm=256):
    n, d = x.shape
    n_exp = w_router.shape[1]
    assert n_exp <= LANES and n_exp % N_GROUPS == 0
    tm = _pick(n, tm)
    wr = jnp.pad(w_router, ((0, 0), (0, LANES - n_exp))).astype(BF16)
    rb = jnp.pad(r_bias, (0, LANES - n_exp)).reshape(1, LANES)
    row = lambda i: (i, 0)
    fix = lambda i: (0, 0)
    return pl.pallas_call(
        functools.partial(_router_kernel, n_exp=n_exp),
        grid=(n // tm,),
        in_specs=[pl.BlockSpec((tm, d), row), pl.BlockSpec((d, LANES), fix), pl.BlockSpec((1, LANES), fix)],
        out_specs=[pl.BlockSpec((tm, LANES), row), pl.BlockSpec((tm, LANES), row),
                   pl.BlockSpec((tm, LANES), row), pl.BlockSpec((1, LANES), fix)],
        out_shape=[jax.ShapeDtypeStruct((n, LANES), jnp.int32), jax.ShapeDtypeStruct((n, LANES), F32),
                   jax.ShapeDtypeStruct((n, LANES), jnp.int32), jax.ShapeDtypeStruct((1, LANES), jnp.int32)],
        scratch_shapes=[pltpu.VMEM((1, LANES), F32)],
        compiler_params=_params(("arbitrary",)),
        name="router",
    )(x, wr, rb)


def expert_plan(idx, rank, counts, n_exp, tm):
    n_tok = idx.shape[0]
    n_asg = n_tok * TOP_K
    n_tiles = n_asg // tm
    n_items = n_tiles + n_exp - 1
    counts = counts[0, :n_exp]
    ends = jnp.cumsum(counts)
    starts = ends - counts
    first_blk = starts // tm
    n_items_e = jnp.where(counts > 0, (ends - 1) // tm - first_blk + 1, 0)
    item_end = jnp.cumsum(n_items_e)
    item_start = item_end - n_items_e
    n_live = item_end[-1]

    e_ids = jnp.arange(n_exp, dtype=jnp.int32)
    mine = idx[:, :TOP_K, None] == e_ids
    lookup = lambda table: jnp.sum(jnp.where(mine, table, 0), axis=-1)
    pos = rank[:, :TOP_K] + lookup(starts)
    pos_out = (lookup(item_start - first_blk) + pos // tm) * tm + pos % tm

    it = jnp.minimum(jnp.arange(n_items, dtype=jnp.int32), n_live - 1)
    onehot = (jnp.sum(item_end[None, :] <= it[:, None], axis=1)[:, None] == e_ids).astype(jnp.int32)
    pick = lambda v: jnp.sum(onehot * v[None, :], axis=1)
    item_exp = pick(e_ids)
    item_blk = jnp.clip(pick(first_blk) + it - pick(item_start), 0, n_tiles - 1)
    new_exp = jnp.concatenate([jnp.ones((1,), jnp.int32), (item_exp[1:] != item_exp[:-1]).astype(jnp.int32)])
    tables = tuple(t.astype(jnp.int32) for t in (item_blk, item_exp, new_exp, n_live.reshape(1)))
    return pos.astype(jnp.int32), pos_out.astype(jnp.int32), tables


def _dispatch_kernel(pos_ref, x_ref, xs_hbm, sem):
    tt = x_ref.shape[0]
    for t in range(tt):
        for k in range(TOP_K):
            pltpu.make_async_copy(x_ref.at[pl.ds(t, 1)], xs_hbm.at[pl.ds(pos_ref[0, 0, t * TOP_K + k], 1)],
                                  sem).start(priority=k % 2)
    for _ in range(TOP_K):
        pltpu.make_async_copy(x_ref, xs_hbm.at[pl.ds(0, tt)], sem).wait()


def dispatch(x, pos, tt=128):
    n, d = x.shape
    tt = _pick(n, tt)
    pos3 = pos.reshape(n // tt, 1, tt * TOP_K)
    return pl.pallas_call(
        _dispatch_kernel,
        grid=(n // tt,),
        in_specs=[pl.BlockSpec((1, 1, tt * TOP_K), lambda i: (i, 0, 0), memory_space=pltpu.SMEM),
                  pl.BlockSpec((tt, d), lambda i: (i, 0))],
        out_specs=pl.BlockSpec(memory_space=pl.ANY),
        out_shape=jax.ShapeDtypeStruct((n * TOP_K, d), x.dtype),
        scratch_shapes=[pltpu.SemaphoreType.DMA],
        compiler_params=_params(("arbitrary",)),
        name="dispatch",
    )(pos3, x)


def _experts_kernel(blk_ref, exp_ref, newexp_ref, nlive_ref, xs_ref, wg_ref, wu_ref, wd_ref, y_ref, wgb, wub, wdb):
    i = pl.program_id(0)

    @pl.when(i < nlive_ref[0])
    def _():
        @pl.when(newexp_ref[i] == 1)
        def _():
            wgb[...] = wg_ref[...].astype(BF16)
            wub[...] = wu_ref[...].astype(BF16)
            wdb[...] = wd_ref[...].astype(BF16)

        x_lo, x_hi = _unpack_halves(xs_ref[...])
        x_lo, x_hi = x_lo.astype(BF16), x_hi.astype(BF16)
        half = x_lo.shape[1]
        g = _dot(x_lo, wgb[0:half, :]) + _dot(x_hi, wgb[half:, :])
        u = _dot(x_lo, wub[0:half, :]) + _dot(x_hi, wub[half:, :])
        h = (g * jax.nn.sigmoid(g) * u).astype(BF16)
        y_ref[...] = _pack_halves(_dot(h, wdb[:, 0:half]), _dot(h, wdb[:, half:]))

    @pl.when(i >= nlive_ref[0])
    def _():
        y_ref[...] = jnp.zeros_like(y_ref)


def routed_experts(xs, tables, w_gate, w_up, w_down, layer, tm):
    n_asg, dp = xs.shape
    d = 2 * dp
    n_exp, f = w_gate.shape[1], w_gate.shape[3]
    n_items = n_asg // tm + n_exp - 1
    w_in = pl.BlockSpec((None, None, d, f), lambda i, blk, ex, *_: (layer, ex[i], 0, 0))
    w_out = pl.BlockSpec((None, None, f, d), lambda i, blk, ex, *_: (layer, ex[i], 0, 0))
    grid_spec = pltpu.PrefetchScalarGridSpec(
        num_scalar_prefetch=4,
        grid=(n_items,),
        in_specs=[pl.BlockSpec((tm, dp), lambda i, blk, *_: (blk[i], 0)), w_in, w_in, w_out],
        out_specs=pl.BlockSpec((tm, dp), lambda i, *_: (i, 0)),
        scratch_shapes=[pltpu.VMEM((d, f), BF16), pltpu.VMEM((d, f), BF16), pltpu.VMEM((f, d), BF16)],
    )
    return pl.pallas_call(
        _experts_kernel,
        grid_spec=grid_spec,
        out_shape=jax.ShapeDtypeStruct((n_items * tm, dp), U32),
        compiler_params=_params(("arbitrary",)),
        name="routed_experts",
    )(*tables, xs, w_gate, w_up, w_down)


def _combine_kernel(pos_ref, posn_ref, y_hbm, wt_ref, xb_ref, sg_ref, su_ref, sd_ref, res_ref, g_ref, b_ref,
                    o_ref, ob_ref, ybuf, sems):
    i = pl.program_id(0)
    n_steps = pl.num_programs(0)
    tc = res_ref.shape[0]

    def gather_wait(s):
        for k in range(TOP_K):
            pltpu.make_async_copy(y_hbm.at[pl.ds(0, tc)], ybuf.at[s, k], sems.at[s]).wait()

    @pl.when(i == 0)
    def _():
        def token(t, c):
            for k in range(TOP_K):
                pltpu.make_async_copy(y_hbm.at[pl.ds(pos_ref[0, 0, t * TOP_K + k], 1)],
                                      ybuf.at[0, k, pl.ds(t, 1)], sems.at[0]).start(priority=k % 2)
            return c
        lax.fori_loop(0, tc, token, 0)

    def step(slot):
        nxt = 1 - slot
        for t in range(tc):
            for k in range(TOP_K):
                pltpu.make_async_copy(y_hbm.at[pl.ds(posn_ref[0, 0, t * TOP_K + k], 1)],
                                      ybuf.at[nxt, k, pl.ds(t, 1)], sems.at[nxt]).start(priority=k % 2)

        xb = xb_ref[...]
        hg = _dot(xb, sg_ref[...])
        hu = _dot(xb, su_ref[...])
        shared = _dot((hg * jax.nn.sigmoid(hg) * hu).astype(BF16), sd_ref[...])
        base = DN_ALPHA * res_ref[...] + shared

        gather_wait(slot)
        wts = wt_ref[...]
        r_lo, r_hi = _unpack_halves(ybuf[slot, 0])
        r_lo, r_hi = r_lo * wts[:, 0:1], r_hi * wts[:, 0:1]
        for k in range(1, TOP_K):
            y_lo, y_hi = _unpack_halves(ybuf[slot, k])
            r_lo, r_hi = r_lo + y_lo * wts[:, k:k + 1], r_hi + y_hi * wts[:, k:k + 1]
        routed = jnp.concatenate([r_lo, r_hi], axis=1)
        z = _layer_norm(base + routed, g_ref[...], b_ref[...])
        o_ref[...] = z
        ob_ref[...] = z.astype(BF16)

        @pl.when(i == n_steps - 1)
        def _():
            gather_wait(nxt)

    for slot in range(2):
        pl.when(i % 2 == slot)(functools.partial(step, slot))


def combine_shared_ln(y, pos, wts, xb, s_gate, s_up, s_down, res, g, b, tc=128):
    n, d = res.shape
    f = s_gate.shape[1]
    tc = _pick(n, tc)
    n_steps = n // tc
    pos3 = pos.reshape(n_steps, 1, tc * TOP_K)
    row = lambda i: (i, 0)
    fix = lambda i: (0, 0)
    pos_blk = lambda imap: pl.BlockSpec((1, 1, tc * TOP_K), imap, memory_space=pltpu.SMEM)
    return pl.pallas_call(
        _combine_kernel,
        grid=(n_steps,),
        in_specs=[pos_blk(lambda i: (i, 0, 0)),
                  pos_blk(lambda i: (jnp.minimum(i + 1, n_steps - 1), 0, 0)),
                  pl.BlockSpec(memory_space=pl.ANY),
                  pl.BlockSpec((tc, LANES), row), pl.BlockSpec((tc, d), row),
                  pl.BlockSpec((d, f), fix), pl.BlockSpec((d, f), fix), pl.BlockSpec((f, d), fix),
                  pl.BlockSpec((tc, d), row), pl.BlockSpec((1, d), fix), pl.BlockSpec((1, d), fix)],
        out_specs=[pl.BlockSpec((tc, d), row), pl.BlockSpec((tc, d), row)],
        out_shape=[jax.ShapeDtypeStruct((n, d), F32), jax.ShapeDtypeStruct((n, d), BF16)],
        scratch_shapes=[pltpu.VMEM((2, TOP_K, tc, y.shape[1]), y.dtype), pltpu.SemaphoreType.DMA((2,))],
        compiler_params=_params(("arbitrary",)),
        name="combine_shared_ln",
    )(pos3, pos3, y, wts, xb, s_gate, s_up, s_down, res, g.reshape(1, d), b.reshape(1, d))


def moe_ffn_ln(xf, xb, layer, w_router, r_bias, w_gate, w_up, w_down, s_gate, s_up, s_down, g, b, tm=256):
    n_exp = w_router.shape[1]
    tm = _pick(xf.shape[0] * TOP_K, tm)
    idx, wts, rank, counts = router(xb, w_router, r_bias)
    pos, pos_out, tables = expert_plan(idx, rank, counts, n_exp, tm)
    xs = dispatch(pack_rows(xb), pos)
    y = routed_experts(xs, tables, w_gate, w_up, w_down, layer, tm)
    return combine_shared_ln(y, pos_out, wts, xb, s_gate.astype(BF16), s_up.astype(BF16), s_down.astype(BF16),
                             xf, g, b)


def kernel(x, rel_bias, attn_w_qkv, attn_w_o, conv_w_in, conv_b_in, conv_w_dw, conv_b_dw,
           conv_ln_g, conv_ln_b, conv_w_out, conv_b_out, ln_mix_g, ln_mix_b, ln_ffn_g, ln_ffn_b,
           moe_w_router, moe_router_bias, moe_w_gate, moe_w_up, moe_w_down,
           shared_w_gate, shared_w_up, shared_w_down):
    batch, seq, d = x.shape
    n = batch * seq
    depth = ln_mix_g.shape[0]
    nb = seq // MOBA_BLOCK
    assert seq % MOBA_BLOCK == 0 and d % N_HEADS == 0 and depth == DEPTH

    tiles = bias_tiles(rel_bias, N_HEADS, _num_bias_tiles(nb), (d // N_HEADS) ** -0.5)
    xf = x.reshape(n, d)
    xb = xf.astype(BF16)
    zero_bias = jnp.zeros((d,), F32)
    for i in range(depth):
        m = i // 2
        if i % 2 == 0:
            qkv = matmul(xb, attn_w_qkv[m].astype(BF16), BF16)
            a = moba_attention(qkv, tiles, batch, seq)
            xf, xb = proj_res_ln(a, attn_w_o[m].astype(BF16), zero_bias, xf, ln_mix_g[i], ln_mix_b[i])
        else:
            h = glu_proj(xb, conv_w_in[m].astype(BF16), conv_b_in[m])
            a = dwconv_ln_silu(h, conv_w_dw[m], conv_b_dw[m], conv_ln_g[m], conv_ln_b[m], batch, seq)
            xf, xb = proj_res_ln(a, conv_w_out[m].astype(BF16), conv_b_out[m], xf, ln_mix_g[i], ln_mix_b[i])
        xf, xb = moe_ffn_ln(xf, xb, i, moe_w_router[i], moe_router_bias[i], moe_w_gate, moe_w_up, moe_w_down,
                            shared_w_gate[i], shared_w_up[i], shared_w_down[i], ln_ffn_g[i], ln_ffn_b[i])
    return xf.reshape(batch, seq, d)
```

```python
import functools
import math

import jax
import jax.numpy as jnp
from jax import lax
from jax.experimental import pallas as pl
from jax.experimental.pallas import tpu as pltpu

N_HEADS = 16
MOBA_BLOCK = 256
MOBA_TOPK = 3
REL_BUCKETS = 32
REL_MAX_DIST = 2048
CONV_K = 31
TOP_K = 8
N_GROUPS = 8
TOPK_GROUPS = 4
ROUTED_SCALE = 2.5
DEPTH = 4
DN_ALPHA = (2 * DEPTH) ** 0.25
LN_EPS = 1e-5

LANES = 128
SUBLANES = 8
VMEM_LIMIT_BYTES = 56 * 1024 * 1024

NEG_BIG = -1e30
KV_STEP = 2
HEADS_STEP = 4

F32 = jnp.float32
BF16 = jnp.bfloat16


def _params(sem):
    return pltpu.CompilerParams(dimension_semantics=sem, vmem_limit_bytes=VMEM_LIMIT_BYTES)


def _pick(n, pref):
    t = min(pref, n)
    while n % t:
        t //= 2
    return t


def _layer_norm(z, g, b):
    mu = jnp.mean(z, axis=-1, keepdims=True)
    zc = z - mu
    var = jnp.mean(zc * zc, axis=-1, keepdims=True)
    return zc * lax.rsqrt(var + LN_EPS) * g + b


def _dot(a, b):
    return jnp.dot(a, b, preferred_element_type=F32)


def _dot_nt(a, b):
    return lax.dot_general(a, b, (((1,), (1,)), ((), ())), preferred_element_type=F32)


def _mm_kernel(x_ref, w_ref, o_ref):
    o_ref[...] = _dot(x_ref[...], w_ref[...]).astype(o_ref.dtype)


def matmul(x, w, out_dtype, tm=1024, tn=512):
    n, k = x.shape
    m = w.shape[1]
    tm, tn = _pick(n, tm), _pick(m, tn)
    return pl.pallas_call(
        _mm_kernel,
        grid=(n // tm, m // tn),
        in_specs=[pl.BlockSpec((tm, k), lambda i, j: (i, 0)),
                  pl.BlockSpec((k, tn), lambda i, j: (0, j))],
        out_specs=pl.BlockSpec((tm, tn), lambda i, j: (i, j)),
        out_shape=jax.ShapeDtypeStruct((n, m), out_dtype),
        compiler_params=_params(("parallel", "arbitrary")),
        name="matmul",
    )(x, w)


def _glu_kernel(x_ref, wa_ref, wg_ref, ba_ref, bg_ref, o_ref):
    x = x_ref[...]
    a = _dot(x, wa_ref[...]) + ba_ref[...]
    g = _dot(x, wg_ref[...]) + bg_ref[...]
    o_ref[...] = a * jax.nn.sigmoid(g)


def glu_proj(x, w, b, tm=1024, tn=256):
    n, k = x.shape
    d = w.shape[1] // 2
    tm, tn = _pick(n, tm), _pick(d, tn)
    nj = d // tn
    b2 = b.reshape(1, 2 * d)
    return pl.pallas_call(
        _glu_kernel,
        grid=(n // tm, nj),
        in_specs=[pl.BlockSpec((tm, k), lambda i, j: (i, 0)),
                  pl.BlockSpec((k, tn), lambda i, j: (0, j)),
                  pl.BlockSpec((k, tn), lambda i, j: (0, j + nj)),
                  pl.BlockSpec((1, tn), lambda i, j: (0, j)),
                  pl.BlockSpec((1, tn), lambda i, j: (0, j + nj))],
        out_specs=pl.BlockSpec((tm, tn), lambda i, j: (i, j)),
        out_shape=jax.ShapeDtypeStruct((n, d), F32),
        compiler_params=_params(("parallel", "arbitrary")),
        name="glu_proj",
    )(x, w, w, b2, b2)


def _proj_res_ln_kernel(a_ref, w_ref, bias_ref, res_ref, g_ref, b_ref, o_ref, ob_ref):
    y = _dot(a_ref[...], w_ref[...]) + bias_ref[...]
    z = _layer_norm(DN_ALPHA * res_ref[...] + y, g_ref[...], b_ref[...])
    o_ref[...] = z
    ob_ref[...] = z.astype(BF16)


def proj_res_ln(a, w, bias, res, g, b, tm=256):
    n, k = a.shape
    d = w.shape[1]
    tm = _pick(n, tm)
    row = lambda i: (i, 0)
    fix = lambda i: (0, 0)
    return pl.pallas_call(
        _proj_res_ln_kernel,
        grid=(n // tm,),
        in_specs=[pl.BlockSpec((tm, k), row), pl.BlockSpec((k, d), fix),
                  pl.BlockSpec((1, d), fix), pl.BlockSpec((tm, d), row),
                  pl.BlockSpec((1, d), fix), pl.BlockSpec((1, d), fix)],
        out_specs=[pl.BlockSpec((tm, d), row), pl.BlockSpec((tm, d), row)],
        out_shape=[jax.ShapeDtypeStruct((n, d), F32), jax.ShapeDtypeStruct((n, d), BF16)],
        compiler_params=_params(("parallel",)),
        name="proj_res_ln",
    )(a, w, bias.reshape(1, d), res, g.reshape(1, d), b.reshape(1, d))


def _t5_bucket(rel):
    n = jnp.maximum(rel, 0)
    max_exact = REL_BUCKETS // 2
    nf = jnp.maximum(n, 1).astype(F32)
    large = max_exact + (jnp.log(nf / max_exact) / math.log(REL_MAX_DIST / max_exact)
                         * (REL_BUCKETS - max_exact)).astype(jnp.int32)
    large = jnp.minimum(large, REL_BUCKETS - 1)
    return jnp.where(n < max_exact, n, large)


def _bias_tile_kernel(rb_ref, o_ref, *, inv_scale):
    h = pl.program_id(0)
    d = pl.program_id(1)
    blk = o_ref.shape[-1]
    row = lax.broadcasted_iota(jnp.int32, (blk, blk), 0)
    col = lax.broadcasted_iota(jnp.int32, (blk, blk), 1)
    rel = d * blk + row - col
    bucket = _t5_bucket(rel)
    acc = jnp.zeros((blk, blk), F32)
    for k in range(REL_BUCKETS):
        acc = jnp.where(bucket == k, rb_ref[k, h], acc)
    o_ref[0, 0] = jnp.where(rel >= 0, acc * inv_scale, NEG_BIG)


def _num_bias_tiles(nb):
    last_start = (REL_MAX_DIST / (REL_BUCKETS // 2)) ** ((REL_BUCKETS // 2 - 1) / (REL_BUCKETS // 2)) \
        * (REL_BUCKETS // 2)
    d = 1
    while (d - 1) * MOBA_BLOCK + 1 < 1.05 * last_start + 1:
        d += 1
    return min(nb, d + 1)


def bias_tiles(rel_bias, n_heads, n_tiles, scale):
    return pl.pallas_call(
        functools.partial(_bias_tile_kernel, inv_scale=1.0 / scale),
        grid=(n_heads, n_tiles),
        in_specs=[pl.BlockSpec(memory_space=pltpu.SMEM)],
        out_specs=pl.BlockSpec((1, 1, MOBA_BLOCK, MOBA_BLOCK), lambda h, d: (h, d, 0, 0)),
        out_shape=jax.ShapeDtypeStruct((n_heads, n_tiles, MOBA_BLOCK, MOBA_BLOCK), F32),
        compiler_params=_params(("parallel", "parallel")),
        name="bias_tiles",
    )(rel_bias)


def _moba_kernel(q_ref, k_ref, v_ref, bias_ref, o_ref, kmean_sc, kaug_sc, *, nb, n_bias, scale):
    j = pl.program_id(2)
    blk = MOBA_BLOCK
    dh = q_ref.shape[2] // HEADS_STEP
    seq = k_ref.shape[1]
    c_exp = scale * math.log2(math.e)
    cw = KV_STEP * blk

    @pl.when(j == 0)
    def _():
        kmean_sc[...] = jnp.zeros_like(kmean_sc)
        for n in range(nb):
            kn = k_ref[0, n * blk:(n + 1) * blk, :].astype(F32)
            kmean_sc[n:n + 1, :] = jnp.mean(kn, axis=0, keepdims=True)
        blk_of_row = lax.broadcasted_iota(jnp.int32, (seq, LANES), 0) // blk
        blk_lane = lax.broadcasted_iota(jnp.int32, (seq, LANES), 1)
        onehot = jnp.where(blk_of_row == blk_lane, 1.0, 0.0).astype(BF16)
        for h in range(HEADS_STEP):
            kaug_sc[h, :, 0:dh] = k_ref[0, :, h * dh:(h + 1) * dh]
            kaug_sc[h, :, dh:dh + LANES] = onehot

    lane = lax.broadcasted_iota(jnp.int32, (blk, LANES), 1)
    lane_f = lane.astype(F32)
    j0 = pl.multiple_of(j * blk, blk)

    q_aug, state = [], []
    for h in range(HEADS_STEP):
        q = q_ref[0, :, h * dh:(h + 1) * dh]
        gate = _dot_nt(q, kmean_sc[:, h * dh:(h + 1) * dh].astype(BF16))
        gate = jnp.where(lane < j, gate, -jnp.inf)
        sel = jnp.zeros((blk, LANES), F32)
        for _ in range(MOBA_TOPK):
            mx = jnp.max(gate, axis=1, keepdims=True)
            first = jnp.min(jnp.where(gate == mx, lane_f, float(LANES)), axis=1, keepdims=True)
            hit = lane_f == first
            sel = jnp.where(hit, jnp.where(mx > -jnp.inf, 1.0, sel), sel)
            gate = jnp.where(hit, -jnp.inf, gate)
        selneg = jnp.where(sel > 0.0, 0.0, NEG_BIG).astype(BF16)
        q_aug.append(jnp.concatenate([q, selneg], axis=1))

        t = _dot_nt(q, k_ref[0, pl.ds(j0, blk), h * dh:(h + 1) * dh]) + bias_ref[h, 0]
        m0 = jnp.max(t, axis=1, keepdims=True)
        p = jnp.exp2((t - m0) * c_exp)
        l0 = jnp.sum(p, axis=1, keepdims=True)
        acc0 = _dot(p.astype(BF16), v_ref[0, pl.ds(j0, blk), h * dh:(h + 1) * dh])
        state.append((m0, l0, acc0))

    def body(c, carry):
        c0 = pl.multiple_of(c * cw, cw)
        out = []
        for h in range(HEADS_STEP):
            m, l, acc = carry[h]
            bias = jnp.concatenate(
                [bias_ref[h, jnp.clip(j - (c * KV_STEP + u), 0, n_bias - 1)] for u in range(KV_STEP)], axis=1)
            t = _dot_nt(q_aug[h], kaug_sc[h, pl.ds(c0, cw), :]) + bias
            m_new = jnp.maximum(m, jnp.max(t, axis=1, keepdims=True))
            alpha = jnp.exp2((m - m_new) * c_exp)
            p = jnp.exp2((t - m_new) * c_exp)
            l = alpha * l + jnp.sum(p, axis=1, keepdims=True)
            acc = alpha * acc + _dot(p.astype(BF16), v_ref[0, pl.ds(c0, cw), h * dh:(h + 1) * dh])
            out.append((m_new, l, acc))
        return tuple(out)

    state = lax.fori_loop(0, (j + KV_STEP - 1) // KV_STEP, body, tuple(state))
    o_ref[0] = jnp.concatenate([acc / l for (_, l, acc) in state], axis=1).astype(o_ref.dtype)


def moba_attention(qkv, tiles, batch, seq):
    d3 = qkv.shape[1]
    d = d3 // 3
    dh = d // N_HEADS
    nb = seq // MOBA_BLOCK
    assert nb % KV_STEP == 0 and N_HEADS % HEADS_STEP == 0 and dh == LANES
    n_bias = tiles.shape[1]
    hp = N_HEADS // HEADS_STEP
    wh = HEADS_STEP * dh
    qkv3 = qkv.reshape(batch, seq, d3)
    kern = functools.partial(_moba_kernel, nb=nb, n_bias=n_bias, scale=dh ** -0.5)
    out = pl.pallas_call(
        kern,
        grid=(batch, hp, nb),
        in_specs=[pl.BlockSpec((1, MOBA_BLOCK, wh), lambda b, h, j: (b, j, h)),
                  pl.BlockSpec((1, seq, wh), lambda b, h, j: (b, 0, hp + h)),
                  pl.BlockSpec((1, seq, wh), lambda b, h, j: (b, 0, 2 * hp + h)),
                  pl.BlockSpec((HEADS_STEP, n_bias, MOBA_BLOCK, MOBA_BLOCK), lambda b, h, j: (h, 0, 0, 0))],
        out_specs=pl.BlockSpec((1, MOBA_BLOCK, wh), lambda b, h, j: (b, j, h)),
        out_shape=jax.ShapeDtypeStruct((batch, seq, d), BF16),
        scratch_shapes=[pltpu.VMEM((LANES, wh), F32), pltpu.VMEM((HEADS_STEP, seq, dh + LANES), BF16)],
        compiler_params=_params(("parallel", "parallel", "arbitrary")),
        name="moba_attention",
    )(qkv3, qkv3, qkv3, tiles)
    return out.reshape(batch * seq, d)


def _dwconv_ln_kernel(prev_ref, cur_ref, w_ref, bdw_ref, g_ref, b_ref, o_ref, buf, conv_sc, hp_sc, *, halo):
    i = pl.program_id(1)
    ts, d = cur_ref.shape[1], cur_ref.shape[2]
    prev = prev_ref[0]
    buf[0:halo, :] = jnp.where(i > 0, prev, jnp.zeros_like(prev))
    buf[halo:halo + ts, :] = cur_ref[0]
    first = halo - (CONV_K - 1)
    for c in range(d // LANES):
        cs = slice(c * LANES, (c + 1) * LANES)
        acc = jnp.zeros((ts, LANES), F32) + bdw_ref[:, cs]
        for p in range(SUBLANES):
            span = (ts + halo - p) // SUBLANES * SUBLANES
            hp_sc[0:span, :] = buf[p:p + span, cs]
            for a in range(span // SUBLANES):
                k = a * SUBLANES + p - first
                if 0 <= k < CONV_K and a * SUBLANES + ts <= span:
                    acc = acc + hp_sc[a * SUBLANES:a * SUBLANES + ts, :] * w_ref[k:k + 1, cs]
        conv_sc[:, cs] = acc
    y = _layer_norm(conv_sc[...], g_ref[...], b_ref[...])
    o_ref[0] = (y * jax.nn.sigmoid(y)).astype(o_ref.dtype)


def dwconv_ln_silu(h, w_dw, b_dw, g, b, batch, seq, ts=256):
    d = h.shape[1]
    ts = _pick(seq, ts)
    halo = 32
    assert halo >= CONV_K - 1 and ts % halo == 0 and d % LANES == 0
    r = ts // halo
    h3 = h.reshape(batch, seq, d)
    fix = lambda bi, i: (0, 0)
    kern = functools.partial(_dwconv_ln_kernel, halo=halo)
    out = pl.pallas_call(
        kern,
        grid=(batch, seq // ts),
        in_specs=[pl.BlockSpec((1, halo, d), lambda bi, i: (bi, jnp.maximum(i * r - 1, 0), 0)),
                  pl.BlockSpec((1, ts, d), lambda bi, i: (bi, i, 0)),
                  pl.BlockSpec((CONV_K, d), fix), pl.BlockSpec((1, d), fix),
                  pl.BlockSpec((1, d), fix), pl.BlockSpec((1, d), fix)],
        out_specs=pl.BlockSpec((1, ts, d), lambda bi, i: (bi, i, 0)),
        out_shape=jax.ShapeDtypeStruct((batch, seq, d), BF16),
        scratch_shapes=[pltpu.VMEM((halo + ts, d), F32), pltpu.VMEM((ts, d), F32),
                        pltpu.VMEM((halo + ts, LANES), F32)],
        compiler_params=_params(("parallel", "parallel")),
        name="dwconv_ln_silu",
    )(h3, h3, w_dw, b_dw.reshape(1, d), g.reshape(1, d), b.reshape(1, d))
    return out.reshape(batch * seq, d)


def _router_kernel(x_ref, wr_ref, rb_ref, idx_ref, wt_ref, rank_ref, cnt_ref, carry, *, n_exp):
    step = pl.program_id(0)
    tm = x_ref.shape[0]
    per_g = n_exp // N_GROUPS
    lane = lax.broadcasted_iota(jnp.int32, (tm, LANES), 1)
    lane_f = lane.astype(F32)
    grp_f = (lane // per_g).astype(F32)
    valid = lane < n_exp
    logits = _dot(x_ref[...], wr_ref[...])
    scores = jax.nn.sigmoid(logits)
    biased = jnp.where(valid, scores + rb_ref[...], -jnp.inf)

    def first_argmax(v):
        mx = jnp.max(v, axis=1, keepdims=True)
        first = jnp.min(jnp.where(v == mx, lane_f, float(LANES)), axis=1, keepdims=True)
        return mx, first

    gscore = jnp.full((tm, LANES), -jnp.inf, F32)
    for g in range(N_GROUPS):
        vg = jnp.where(grp_f == float(g), biased, -jnp.inf)
        m1, f1 = first_argmax(vg)
        m2 = jnp.max(jnp.where(lane_f == f1, -jnp.inf, vg), axis=1, keepdims=True)
        gscore = jnp.where(lane == g, m1 + m2, gscore)
    keep = jnp.zeros((tm, LANES), F32)
    for _ in range(TOPK_GROUPS):
        _, fg = first_argmax(gscore)
        keep = jnp.where(grp_f == fg, 1.0, keep)
        gscore = jnp.where(lane_f == fg, -jnp.inf, gscore)
    cand = jnp.where(keep > 0.0, biased, -jnp.inf)
    idx = jnp.zeros((tm, LANES), F32)
    wts = jnp.zeros((tm, LANES), F32)
    chosen = jnp.zeros((tm, LANES), F32)
    picks = []
    for k in range(TOP_K):
        _, fe = first_argmax(cand)
        hit = lane_f == fe
        wk = jnp.sum(jnp.where(hit, scores, 0.0), axis=1, keepdims=True)
        idx = jnp.where(lane == k, fe, idx)
        wts = jnp.where(lane == k, wk, wts)
        chosen = jnp.where(hit, 1.0, chosen)
        cand = jnp.where(hit, -jnp.inf, cand)
        picks.append(fe)
    wsum = jnp.sum(wts, axis=1, keepdims=True)
    idx_ref[...] = idx.astype(jnp.int32)
    wt_ref[...] = wts / wsum * ROUTED_SCALE

    @pl.when(step == 0)
    def _():
        carry[...] = jnp.zeros_like(carry)

    r_i = lax.broadcasted_iota(jnp.int32, (tm, tm), 0)
    c_i = lax.broadcasted_iota(jnp.int32, (tm, tm), 1)
    earlier = jnp.where(r_i > c_i, 1.0, 0.0).astype(BF16)
    rank = _dot(earlier, chosen.astype(BF16)) + carry[...]
    rank_sel = jnp.zeros((tm, LANES), F32)
    for k in range(TOP_K):
        rk = jnp.sum(jnp.where(lane_f == picks[k], rank, 0.0), axis=1, keepdims=True)
        rank_sel = jnp.where(lane == k, rk, rank_sel)
    rank_ref[...] = rank_sel.astype(jnp.int32)
    total = carry[...] + jnp.sum(chosen, axis=0, keepdims=True)
    carry[...] = total
    cnt_ref[...] = total.astype(jnp.int32)


def router(x, w_router, r_bias, tm=256):
    n, d = x.shape
    n_exp = w_router.shape[1]
    assert n_exp <= LANES and n_exp % N_GROUPS == 0
    tm = _pick(n, tm)
    wr = jnp.pad(w_router, ((0, 0), (0, LANES - n_exp))).astype(BF16)
    rb = jnp.pad(r_bias, (0, LANES - n_exp)).reshape(1, LANES)
    row = lambda i: (i, 0)
    fix = lambda i: (0, 0)
    return pl.pallas_call(
        functools.partial(_router_kernel, n_exp=n_exp),
        grid=(n // tm,),
        in_specs=[pl.BlockSpec((tm, d), row), pl.BlockSpec((d, LANES), fix), pl.BlockSpec((1, LANES), fix)],
        out_specs=[pl.BlockSpec((tm, LANES), row), pl.BlockSpec((tm, LANES), row),
                   pl.BlockSpec((tm, LANES), row), pl.BlockSpec((1, LANES), fix)],
        out_shape=[jax.ShapeDtypeStruct((n, LANES), jnp.int32), jax.ShapeDtypeStruct((n, LANES), F32),
                   jax.ShapeDtypeStruct((n, LANES), jnp.int32), jax.ShapeDtypeStruct((1, LANES), jnp.int32)],
        scratch_shapes=[pltpu.VMEM((1, LANES), F32)],
        compiler_params=_params(("arbitrary",)),
        name="router",
    )(x, wr, rb)


def expert_plan(idx, rank, counts, n_exp, tm):
    n_tok = idx.shape[0]
    n_asg = n_tok * TOP_K
    n_tiles = n_asg // tm
    n_items = n_tiles + n_exp - 1
    counts = counts[0, :n_exp]
    ends = jnp.cumsum(counts)
    starts = ends - counts
    first_blk = starts // tm
    n_items_e = jnp.where(counts > 0, (ends - 1) // tm - first_blk + 1, 0)
    item_end = jnp.cumsum(n_items_e)
    item_start = item_end - n_items_e
    n_live = item_end[-1]

    e_ids = jnp.arange(n_exp, dtype=jnp.int32)
    mine = idx[:, :TOP_K, None] == e_ids
    lookup = lambda table: jnp.sum(jnp.where(mine, table, 0), axis=-1)
    pos = rank[:, :TOP_K] + lookup(starts)
    pos_out = (lookup(item_start - first_blk) + pos // tm) * tm + pos % tm

    it = jnp.minimum(jnp.arange(n_items, dtype=jnp.int32), n_live - 1)
    onehot = (jnp.sum(item_end[None, :] <= it[:, None], axis=1)[:, None] == e_ids).astype(jnp.int32)
    pick = lambda v: jnp.sum(onehot * v[None, :], axis=1)
    item_exp = pick(e_ids)
    item_blk = jnp.clip(pick(first_blk) + it - pick(item_start), 0, n_tiles - 1)
    new_exp = jnp.concatenate([jnp.ones((1,), jnp.int32), (item_exp[1:] != item_exp[:-1]).astype(jnp.int32)])
    tables = tuple(t.astype(jnp.int32) for t in (item_blk, item_exp, new_exp, n_live.reshape(1)))
    return pos.astype(jnp.int32), pos_out.astype(jnp.int32), tables


def _dispatch_kernel(pos_ref, x_ref, xs_hbm, sem):
    tt = x_ref.shape[0]
    for t in range(tt):
        for k in range(TOP_K):
            pltpu.make_async_copy(x_ref.at[pl.ds(t, 1)], xs_hbm.at[pl.ds(pos_ref[0, 0, t * TOP_K + k], 1)],
                                  sem).start(priority=k % 2)
    for _ in range(TOP_K):
        pltpu.make_async_copy(x_ref, xs_hbm.at[pl.ds(0, tt)], sem).wait()


def dispatch(x, pos, tt=128):
    n, d = x.shape
    tt = _pick(n, tt)
    pos3 = pos.reshape(n // tt, 1, tt * TOP_K)
    return pl.pallas_call(
        _dispatch_kernel,
        grid=(n // tt,),
        in_specs=[pl.BlockSpec((1, 1, tt * TOP_K), lambda i: (i, 0, 0), memory_space=pltpu.SMEM),
                  pl.BlockSpec((tt, d), lambda i: (i, 0))],
        out_specs=pl.BlockSpec(memory_space=pl.ANY),
        out_shape=jax.ShapeDtypeStruct((n * TOP_K, d), x.dtype),
        scratch_shapes=[pltpu.SemaphoreType.DMA],
        compiler_params=_params(("arbitrary",)),
        name="dispatch",
    )(pos3, x)


def _experts_kernel(blk_ref, exp_ref, newexp_ref, nlive_ref, xs_ref, wg_ref, wu_ref, wd_ref, y_ref, wgb, wub, wdb):
    i = pl.program_id(0)

    @pl.when(i < nlive_ref[0])
    def _():
        @pl.when(newexp_ref[i] == 1)
        def _():
            wgb[...] = wg_ref[...].astype(BF16)
            wub[...] = wu_ref[...].astype(BF16)
            wdb[...] = wd_ref[...].astype(BF16)

        x = xs_ref[...].astype(BF16)
        g = _dot(x, wgb[...])
        u = _dot(x, wub[...])
        h = (g * jax.nn.sigmoid(g) * u).astype(BF16)
        y_ref[...] = _dot(h, wdb[...])

    @pl.when(i >= nlive_ref[0])
    def _():
        y_ref[...] = jnp.zeros_like(y_ref)


def routed_experts(xs, tables, w_gate, w_up, w_down, layer, tm):
    n_asg, d = xs.shape
    n_exp, f = w_gate.shape[1], w_gate.shape[3]
    n_items = n_asg // tm + n_exp - 1
    w_in = pl.BlockSpec((None, None, d, f), lambda i, blk, ex, *_: (layer, ex[i], 0, 0))
    w_out = pl.BlockSpec((None, None, f, d), lambda i, blk, ex, *_: (layer, ex[i], 0, 0))
    grid_spec = pltpu.PrefetchScalarGridSpec(
        num_scalar_prefetch=4,
        grid=(n_items,),
        in_specs=[pl.BlockSpec((tm, d), lambda i, blk, *_: (blk[i], 0)), w_in, w_in, w_out],
        out_specs=pl.BlockSpec((tm, d), lambda i, *_: (i, 0)),
        scratch_shapes=[pltpu.VMEM((d, f), BF16), pltpu.VMEM((d, f), BF16), pltpu.VMEM((f, d), BF16)],
    )
    return pl.pallas_call(
        _experts_kernel,
        grid_spec=grid_spec,
        out_shape=jax.ShapeDtypeStruct((n_items * tm, d), F32),
        compiler_params=_params(("arbitrary",)),
        name="routed_experts",
    )(*tables, xs, w_gate, w_up, w_down)


def _combine_kernel(pos_ref, posn_ref, y_hbm, wt_ref, xb_ref, sg_ref, su_ref, sd_ref, res_ref, g_ref, b_ref,
                    o_ref, ob_ref, ybuf, sems):
    i = pl.program_id(0)
    n_steps = pl.num_programs(0)
    tc = res_ref.shape[0]

    def gather_wait(s):
        for k in range(TOP_K):
            pltpu.make_async_copy(y_hbm.at[pl.ds(0, tc)], ybuf.at[s, k], sems.at[s]).wait()

    @pl.when(i == 0)
    def _():
        def token(t, c):
            for k in range(TOP_K):
                pltpu.make_async_copy(y_hbm.at[pl.ds(pos_ref[0, 0, t * TOP_K + k], 1)],
                                      ybuf.at[0, k, pl.ds(t, 1)], sems.at[0]).start(priority=k % 2)
            return c
        lax.fori_loop(0, tc, token, 0)

    def step(slot):
        nxt = 1 - slot
        for t in range(tc):
            for k in range(TOP_K):
                pltpu.make_async_copy(y_hbm.at[pl.ds(posn_ref[0, 0, t * TOP_K + k], 1)],
                                      ybuf.at[nxt, k, pl.ds(t, 1)], sems.at[nxt]).start(priority=k % 2)

        xb = xb_ref[...]
        hg = _dot(xb, sg_ref[...])
        hu = _dot(xb, su_ref[...])
        shared = _dot((hg * jax.nn.sigmoid(hg) * hu).astype(BF16), sd_ref[...])
        base = DN_ALPHA * res_ref[...] + shared

        gather_wait(slot)
        wts = wt_ref[...]
        routed = ybuf[slot, 0] * wts[:, 0:1]
        for k in range(1, TOP_K):
            routed = routed + ybuf[slot, k] * wts[:, k:k + 1]
        z = _layer_norm(base + routed, g_ref[...], b_ref[...])
        o_ref[...] = z
        ob_ref[...] = z.astype(BF16)

        @pl.when(i == n_steps - 1)
        def _():
            gather_wait(nxt)

    for slot in range(2):
        pl.when(i % 2 == slot)(functools.partial(step, slot))


def combine_shared_ln(y, pos, wts, xb, s_gate, s_up, s_down, res, g, b, tc=128):
    n, d = res.shape
    f = s_gate.shape[1]
    tc = _pick(n, tc)
    n_steps = n // tc
    pos3 = pos.reshape(n_steps, 1, tc * TOP_K)
    row = lambda i: (i, 0)
    fix = lambda i: (0, 0)
    pos_blk = lambda imap: pl.BlockSpec((1, 1, tc * TOP_K), imap, memory_space=pltpu.SMEM)
    return pl.pallas_call(
        _combine_kernel,
        grid=(n_steps,),
        in_specs=[pos_blk(lambda i: (i, 0, 0)),
                  pos_blk(lambda i: (jnp.minimum(i + 1, n_steps - 1), 0, 0)),
                  pl.BlockSpec(memory_space=pl.ANY),
                  pl.BlockSpec((tc, LANES), row), pl.BlockSpec((tc, d), row),
                  pl.BlockSpec((d, f), fix), pl.BlockSpec((d, f), fix), pl.BlockSpec((f, d), fix),
                  pl.BlockSpec((tc, d), row), pl.BlockSpec((1, d), fix), pl.BlockSpec((1, d), fix)],
        out_specs=[pl.BlockSpec((tc, d), row), pl.BlockSpec((tc, d), row)],
        out_shape=[jax.ShapeDtypeStruct((n, d), F32), jax.ShapeDtypeStruct((n, d), BF16)],
        scratch_shapes=[pltpu.VMEM((2, TOP_K, tc, d), F32), pltpu.SemaphoreType.DMA((2,))],
        compiler_params=_params(("arbitrary",)),
        name="combine_shared_ln",
    )(pos3, pos3, y, wts, xb, s_gate, s_up, s_down, res, g.reshape(1, d), b.reshape(1, d))


def moe_ffn_ln(xf, xb, layer, w_router, r_bias, w_gate, w_up, w_down, s_gate, s_up, s_down, g, b, tm=512):
    n_exp = w_router.shape[1]
    tm = _pick(xf.shape[0] * TOP_K, tm)
    idx, wts, rank, counts = router(xb, w_router, r_bias)
    pos, pos_out, tables = expert_plan(idx, rank, counts, n_exp, tm)
    xs = dispatch(xf, pos)
    y = routed_experts(xs, tables, w_gate, w_up, w_down, layer, tm)
    return combine_shared_ln(y, pos_out, wts, xb, s_gate.astype(BF16), s_up.astype(BF16), s_down.astype(BF16),
                             xf, g, b)


def kernel(x, rel_bias, attn_w_qkv, attn_w_o, conv_w_in, conv_b_in, conv_w_dw, conv_b_dw,
           conv_ln_g, conv_ln_b, conv_w_out, conv_b_out, ln_mix_g, ln_mix_b, ln_ffn_g, ln_ffn_b,
           moe_w_router, moe_router_bias, moe_w_gate, moe_w_up, moe_w_down,
           shared_w_gate, shared_w_up, shared_w_down):
    batch, seq, d = x.shape
    n = batch * seq
    depth = ln_mix_g.shape[0]
    nb = seq // MOBA_BLOCK
    assert seq % MOBA_BLOCK == 0 and d % N_HEADS == 0 and depth == DEPTH

    tiles = bias_tiles(rel_bias, N_HEADS, _num_bias_tiles(nb), (d // N_HEADS) ** -0.5)
    xf = x.reshape(n, d)
    xb = xf.astype(BF16)
    zero_bias = jnp.zeros((d,), F32)
    for i in range(depth):
        m = i // 2
        if i % 2 == 0:
            qkv = matmul(xb, attn_w_qkv[m].astype(BF16), BF16)
            a = moba_attention(qkv, tiles, batch, seq)
            xf, xb = proj_res_ln(a, attn_w_o[m].astype(BF16), zero_bias, xf, ln_mix_g[i], ln_mix_b[i])
        else:
            h = glu_proj(xb, conv_w_in[m].astype(BF16), conv_b_in[m])
            a = dwconv_ln_silu(h, conv_w_dw[m], conv_b_dw[m], conv_ln_g[m], conv_ln_b[m], batch, seq)
            xf, xb = proj_res_ln(a, conv_w_out[m].astype(BF16), conv_b_out[m], xf, ln_mix_g[i], ln_mix_b[i])
        xf, xb = moe_ffn_ln(xf, xb, i, moe_w_router[i], moe_router_bias[i], moe_w_gate, moe_w_up, moe_w_down,
                            shared_w_gate[i], shared_w_up[i], shared_w_down[i], ln_ffn_g[i], ln_ffn_b[i])
    return xf.reshape(batch, seq, d)
```

```python
import functools
import math

import jax
import jax.numpy as jnp
from jax import lax
from jax.experimental import pallas as pl
from jax.experimental.pallas import tpu as pltpu

N_HEADS = 16
MOBA_BLOCK = 256
MOBA_TOPK = 3
REL_BUCKETS = 32
REL_MAX_DIST = 2048
CONV_K = 31
TOP_K = 8
N_GROUPS = 8
TOPK_GROUPS = 4
ROUTED_SCALE = 2.5
DEPTH = 4
DN_ALPHA = (2 * DEPTH) ** 0.25
LN_EPS = 1e-5

LANES = 128
SUBLANES = 8
VMEM_LIMIT_BYTES = 56 * 1024 * 1024

NEG_BIG = -1e30
KV_STEP = 4
HEADS_STEP = 4

F32 = jnp.float32
BF16 = jnp.bfloat16


def _params(sem):
    return pltpu.CompilerParams(dimension_semantics=sem, vmem_limit_bytes=VMEM_LIMIT_BYTES)


def _pick(n, pref):
    t = min(pref, n)
    while n % t:
        t //= 2
    return t


def _layer_norm(z, g, b):
    mu = jnp.mean(z, axis=-1, keepdims=True)
    zc = z - mu
    var = jnp.mean(zc * zc, axis=-1, keepdims=True)
    return zc * lax.rsqrt(var + LN_EPS) * g + b


def _dot(a, b):
    return jnp.dot(a, b, preferred_element_type=F32)


def _dot_nt(a, b):
    return lax.dot_general(a, b, (((1,), (1,)), ((), ())), preferred_element_type=F32)


def _mm_kernel(x_ref, w_ref, o_ref):
    o_ref[...] = _dot(x_ref[...], w_ref[...]).astype(o_ref.dtype)


def matmul(x, w, out_dtype, tm=1024, tn=512):
    n, k = x.shape
    m = w.shape[1]
    tm, tn = _pick(n, tm), _pick(m, tn)
    return pl.pallas_call(
        _mm_kernel,
        grid=(n // tm, m // tn),
        in_specs=[pl.BlockSpec((tm, k), lambda i, j: (i, 0)),
                  pl.BlockSpec((k, tn), lambda i, j: (0, j))],
        out_specs=pl.BlockSpec((tm, tn), lambda i, j: (i, j)),
        out_shape=jax.ShapeDtypeStruct((n, m), out_dtype),
        compiler_params=_params(("parallel", "arbitrary")),
        name="matmul",
    )(x, w)


def _glu_kernel(x_ref, wa_ref, wg_ref, ba_ref, bg_ref, o_ref):
    x = x_ref[...]
    a = _dot(x, wa_ref[...]) + ba_ref[...]
    g = _dot(x, wg_ref[...]) + bg_ref[...]
    o_ref[...] = a * jax.nn.sigmoid(g)


def glu_proj(x, w, b, tm=1024, tn=256):
    n, k = x.shape
    d = w.shape[1] // 2
    tm, tn = _pick(n, tm), _pick(d, tn)
    nj = d // tn
    b2 = b.reshape(1, 2 * d)
    return pl.pallas_call(
        _glu_kernel,
        grid=(n // tm, nj),
        in_specs=[pl.BlockSpec((tm, k), lambda i, j: (i, 0)),
                  pl.BlockSpec((k, tn), lambda i, j: (0, j)),
                  pl.BlockSpec((k, tn), lambda i, j: (0, j + nj)),
                  pl.BlockSpec((1, tn), lambda i, j: (0, j)),
                  pl.BlockSpec((1, tn), lambda i, j: (0, j + nj))],
        out_specs=pl.BlockSpec((tm, tn), lambda i, j: (i, j)),
        out_shape=jax.ShapeDtypeStruct((n, d), F32),
        compiler_params=_params(("parallel", "arbitrary")),
        name="glu_proj",
    )(x, w, w, b2, b2)


def _proj_res_ln_kernel(a_ref, w_ref, bias_ref, res_ref, g_ref, b_ref, o_ref, ob_ref):
    y = _dot(a_ref[...], w_ref[...]) + bias_ref[...]
    z = _layer_norm(DN_ALPHA * res_ref[...] + y, g_ref[...], b_ref[...])
    o_ref[...] = z
    ob_ref[...] = z.astype(BF16)


def proj_res_ln(a, w, bias, res, g, b, tm=256):
    n, k = a.shape
    d = w.shape[1]
    tm = _pick(n, tm)
    row = lambda i: (i, 0)
    fix = lambda i: (0, 0)
    return pl.pallas_call(
        _proj_res_ln_kernel,
        grid=(n // tm,),
        in_specs=[pl.BlockSpec((tm, k), row), pl.BlockSpec((k, d), fix),
                  pl.BlockSpec((1, d), fix), pl.BlockSpec((tm, d), row),
                  pl.BlockSpec((1, d), fix), pl.BlockSpec((1, d), fix)],
        out_specs=[pl.BlockSpec((tm, d), row), pl.BlockSpec((tm, d), row)],
        out_shape=[jax.ShapeDtypeStruct((n, d), F32), jax.ShapeDtypeStruct((n, d), BF16)],
        compiler_params=_params(("parallel",)),
        name="proj_res_ln",
    )(a, w, bias.reshape(1, d), res, g.reshape(1, d), b.reshape(1, d))


def _t5_bucket(rel):
    n = jnp.maximum(rel, 0)
    max_exact = REL_BUCKETS // 2
    nf = jnp.maximum(n, 1).astype(F32)
    large = max_exact + (jnp.log(nf / max_exact) / math.log(REL_MAX_DIST / max_exact)
                         * (REL_BUCKETS - max_exact)).astype(jnp.int32)
    large = jnp.minimum(large, REL_BUCKETS - 1)
    return jnp.where(n < max_exact, n, large)


def _bias_tile_kernel(rb_ref, o_ref, *, inv_scale):
    h = pl.program_id(0)
    d = pl.program_id(1)
    blk = o_ref.shape[-1]
    row = lax.broadcasted_iota(jnp.int32, (blk, blk), 0)
    col = lax.broadcasted_iota(jnp.int32, (blk, blk), 1)
    rel = d * blk + row - col
    bucket = _t5_bucket(rel)
    acc = jnp.zeros((blk, blk), F32)
    for k in range(REL_BUCKETS):
        acc = jnp.where(bucket == k, rb_ref[k, h], acc)
    o_ref[0, 0] = jnp.where(rel >= 0, acc * inv_scale, NEG_BIG)


def _num_bias_tiles(nb):
    last_start = (REL_MAX_DIST / (REL_BUCKETS // 2)) ** ((REL_BUCKETS // 2 - 1) / (REL_BUCKETS // 2)) \
        * (REL_BUCKETS // 2)
    d = 1
    while (d - 1) * MOBA_BLOCK + 1 < 1.05 * last_start + 1:
        d += 1
    return min(nb, d + 1)


def bias_tiles(rel_bias, n_heads, n_tiles, scale):
    return pl.pallas_call(
        functools.partial(_bias_tile_kernel, inv_scale=1.0 / scale),
        grid=(n_heads, n_tiles),
        in_specs=[pl.BlockSpec(memory_space=pltpu.SMEM)],
        out_specs=pl.BlockSpec((1, 1, MOBA_BLOCK, MOBA_BLOCK), lambda h, d: (h, d, 0, 0)),
        out_shape=jax.ShapeDtypeStruct((n_heads, n_tiles, MOBA_BLOCK, MOBA_BLOCK), F32),
        compiler_params=_params(("parallel", "parallel")),
        name="bias_tiles",
    )(rel_bias)


def _moba_kernel(q_ref, k_ref, v_ref, bias_ref, o_ref, kmean_sc, kaug_sc, *, nb, n_bias, scale):
    j = pl.program_id(2)
    blk = MOBA_BLOCK
    dh = q_ref.shape[2] // HEADS_STEP
    seq = k_ref.shape[1]
    c_exp = scale * math.log2(math.e)
    cw = KV_STEP * blk

    @pl.when(j == 0)
    def _():
        kmean_sc[...] = jnp.zeros_like(kmean_sc)
        for n in range(nb):
            kn = k_ref[0, n * blk:(n + 1) * blk, :].astype(F32)
            kmean_sc[n:n + 1, :] = jnp.mean(kn, axis=0, keepdims=True)
        blk_of_row = lax.broadcasted_iota(jnp.int32, (seq, LANES), 0) // blk
        blk_lane = lax.broadcasted_iota(jnp.int32, (seq, LANES), 1)
        onehot = jnp.where(blk_of_row == blk_lane, 1.0, 0.0).astype(BF16)
        for h in range(HEADS_STEP):
            kaug_sc[h, :, 0:dh] = k_ref[0, :, h * dh:(h + 1) * dh]
            kaug_sc[h, :, dh:dh + LANES] = onehot

    lane = lax.broadcasted_iota(jnp.int32, (blk, LANES), 1)
    lane_f = lane.astype(F32)
    j0 = pl.multiple_of(j * blk, blk)

    q_aug, state = [], []
    for h in range(HEADS_STEP):
        q = q_ref[0, :, h * dh:(h + 1) * dh]
        gate = _dot_nt(q, kmean_sc[:, h * dh:(h + 1) * dh].astype(BF16))
        gate = jnp.where(lane < j, gate, -jnp.inf)
        sel = jnp.zeros((blk, LANES), F32)
        for _ in range(MOBA_TOPK):
            mx = jnp.max(gate, axis=1, keepdims=True)
            first = jnp.min(jnp.where(gate == mx, lane_f, float(LANES)), axis=1, keepdims=True)
            hit = lane_f == first
            sel = jnp.where(hit, jnp.where(mx > -jnp.inf, 1.0, sel), sel)
            gate = jnp.where(hit, -jnp.inf, gate)
        selneg = jnp.where(sel > 0.0, 0.0, NEG_BIG).astype(BF16)
        q_aug.append(jnp.concatenate([q, selneg], axis=1))

        t = _dot_nt(q, k_ref[0, pl.ds(j0, blk), h * dh:(h + 1) * dh]) + bias_ref[h, 0]
        m0 = jnp.max(t, axis=1, keepdims=True)
        p = jnp.exp2((t - m0) * c_exp)
        l0 = jnp.sum(p, axis=1, keepdims=True)
        acc0 = _dot(p.astype(BF16), v_ref[0, pl.ds(j0, blk), h * dh:(h + 1) * dh])
        state.append((m0, l0, acc0))

    def body(c, carry):
        c0 = pl.multiple_of(c * cw, cw)
        out = []
        for h in range(HEADS_STEP):
            m, l, acc = carry[h]
            bias = jnp.concatenate(
                [bias_ref[h, jnp.clip(j - (c * KV_STEP + u), 0, n_bias - 1)] for u in range(KV_STEP)], axis=1)
            t = _dot_nt(q_aug[h], kaug_sc[h, pl.ds(c0, cw), :]) + bias
            m_new = jnp.maximum(m, jnp.max(t, axis=1, keepdims=True))
            alpha = jnp.exp2((m - m_new) * c_exp)
            p = jnp.exp2((t - m_new) * c_exp)
            l = alpha * l + jnp.sum(p, axis=1, keepdims=True)
            acc = alpha * acc + _dot(p.astype(BF16), v_ref[0, pl.ds(c0, cw), h * dh:(h + 1) * dh])
            out.append((m_new, l, acc))
        return tuple(out)

    state = lax.fori_loop(0, (j + KV_STEP - 1) // KV_STEP, body, tuple(state))
    o_ref[0] = jnp.concatenate([acc / l for (_, l, acc) in state], axis=1).astype(o_ref.dtype)


def moba_attention(qkv, tiles, batch, seq):
    d3 = qkv.shape[1]
    d = d3 // 3
    dh = d // N_HEADS
    nb = seq // MOBA_BLOCK
    assert nb % KV_STEP == 0 and N_HEADS % HEADS_STEP == 0 and dh == LANES
    n_bias = tiles.shape[1]
    hp = N_HEADS // HEADS_STEP
    wh = HEADS_STEP * dh
    qkv3 = qkv.reshape(batch, seq, d3)
    kern = functools.partial(_moba_kernel, nb=nb, n_bias=n_bias, scale=dh ** -0.5)
    out = pl.pallas_call(
        kern,
        grid=(batch, hp, nb),
        in_specs=[pl.BlockSpec((1, MOBA_BLOCK, wh), lambda b, h, j: (b, j, h)),
                  pl.BlockSpec((1, seq, wh), lambda b, h, j: (b, 0, hp + h)),
                  pl.BlockSpec((1, seq, wh), lambda b, h, j: (b, 0, 2 * hp + h)),
                  pl.BlockSpec((HEADS_STEP, n_bias, MOBA_BLOCK, MOBA_BLOCK), lambda b, h, j: (h, 0, 0, 0))],
        out_specs=pl.BlockSpec((1, MOBA_BLOCK, wh), lambda b, h, j: (b, j, h)),
        out_shape=jax.ShapeDtypeStruct((batch, seq, d), BF16),
        scratch_shapes=[pltpu.VMEM((LANES, wh), F32), pltpu.VMEM((HEADS_STEP, seq, dh + LANES), BF16)],
        compiler_params=_params(("parallel", "parallel", "arbitrary")),
        name="moba_attention",
    )(qkv3, qkv3, qkv3, tiles)
    return out.reshape(batch * seq, d)


def _dwconv_ln_kernel(prev_ref, cur_ref, w_ref, bdw_ref, g_ref, b_ref, o_ref, buf, conv_sc, hp_sc, *, halo):
    i = pl.program_id(1)
    ts, d = cur_ref.shape[1], cur_ref.shape[2]
    prev = prev_ref[0]
    buf[0:halo, :] = jnp.where(i > 0, prev, jnp.zeros_like(prev))
    buf[halo:halo + ts, :] = cur_ref[0]
    first = halo - (CONV_K - 1)
    for c in range(d // LANES):
        cs = slice(c * LANES, (c + 1) * LANES)
        acc = jnp.zeros((ts, LANES), F32) + bdw_ref[:, cs]
        for p in range(SUBLANES):
            span = (ts + halo - p) // SUBLANES * SUBLANES
            hp_sc[0:span, :] = buf[p:p + span, cs]
            for a in range(span // SUBLANES):
                k = a * SUBLANES + p - first
                if 0 <= k < CONV_K and a * SUBLANES + ts <= span:
                    acc = acc + hp_sc[a * SUBLANES:a * SUBLANES + ts, :] * w_ref[k:k + 1, cs]
        conv_sc[:, cs] = acc
    y = _layer_norm(conv_sc[...], g_ref[...], b_ref[...])
    o_ref[0] = (y * jax.nn.sigmoid(y)).astype(o_ref.dtype)


def dwconv_ln_silu(h, w_dw, b_dw, g, b, batch, seq, ts=256):
    d = h.shape[1]
    ts = _pick(seq, ts)
    halo = 32
    assert halo >= CONV_K - 1 and ts % halo == 0 and d % LANES == 0
    r = ts // halo
    h3 = h.reshape(batch, seq, d)
    fix = lambda bi, i: (0, 0)
    kern = functools.partial(_dwconv_ln_kernel, halo=halo)
    out = pl.pallas_call(
        kern,
        grid=(batch, seq // ts),
        in_specs=[pl.BlockSpec((1, halo, d), lambda bi, i: (bi, jnp.maximum(i * r - 1, 0), 0)),
                  pl.BlockSpec((1, ts, d), lambda bi, i: (bi, i, 0)),
                  pl.BlockSpec((CONV_K, d), fix), pl.BlockSpec((1, d), fix),
                  pl.BlockSpec((1, d), fix), pl.BlockSpec((1, d), fix)],
        out_specs=pl.BlockSpec((1, ts, d), lambda bi, i: (bi, i, 0)),
        out_shape=jax.ShapeDtypeStruct((batch, seq, d), BF16),
        scratch_shapes=[pltpu.VMEM((halo + ts, d), F32), pltpu.VMEM((ts, d), F32),
                        pltpu.VMEM((halo + ts, LANES), F32)],
        compiler_params=_params(("parallel", "parallel")),
        name="dwconv_ln_silu",
    )(h3, h3, w_dw, b_dw.reshape(1, d), g.reshape(1, d), b.reshape(1, d))
    return out.reshape(batch * seq, d)


def _router_kernel(x_ref, wr_ref, rb_ref, idx_ref, wt_ref, rank_ref, cnt_ref, carry, *, n_exp):
    step = pl.program_id(0)
    tm = x_ref.shape[0]
    per_g = n_exp // N_GROUPS
    lane = lax.broadcasted_iota(jnp.int32, (tm, LANES), 1)
    lane_f = lane.astype(F32)
    grp_f = (lane // per_g).astype(F32)
    valid = lane < n_exp
    logits = _dot(x_ref[...], wr_ref[...])
    scores = jax.nn.sigmoid(logits)
    biased = jnp.where(valid, scores + rb_ref[...], -jnp.inf)

    def first_argmax(v):
        mx = jnp.max(v, axis=1, keepdims=True)
        first = jnp.min(jnp.where(v == mx, lane_f, float(LANES)), axis=1, keepdims=True)
        return mx, first

    gscore = jnp.full((tm, LANES), -jnp.inf, F32)
    for g in range(N_GROUPS):
        vg = jnp.where(grp_f == float(g), biased, -jnp.inf)
        m1, f1 = first_argmax(vg)
        m2 = jnp.max(jnp.where(lane_f == f1, -jnp.inf, vg), axis=1, keepdims=True)
        gscore = jnp.where(lane == g, m1 + m2, gscore)
    keep = jnp.zeros((tm, LANES), F32)
    for _ in range(TOPK_GROUPS):
        _, fg = first_argmax(gscore)
        keep = jnp.where(grp_f == fg, 1.0, keep)
        gscore = jnp.where(lane_f == fg, -jnp.inf, gscore)
    cand = jnp.where(keep > 0.0, biased, -jnp.inf)
    idx = jnp.zeros((tm, LANES), F32)
    wts = jnp.zeros((tm, LANES), F32)
    chosen = jnp.zeros((tm, LANES), F32)
    picks = []
    for k in range(TOP_K):
        _, fe = first_argmax(cand)
        hit = lane_f == fe
        wk = jnp.sum(jnp.where(hit, scores, 0.0), axis=1, keepdims=True)
        idx = jnp.where(lane == k, fe, idx)
        wts = jnp.where(lane == k, wk, wts)
        chosen = jnp.where(hit, 1.0, chosen)
        cand = jnp.where(hit, -jnp.inf, cand)
        picks.append(fe)
    wsum = jnp.sum(wts, axis=1, keepdims=True)
    idx_ref[...] = idx.astype(jnp.int32)
    wt_ref[...] = wts / wsum * ROUTED_SCALE

    @pl.when(step == 0)
    def _():
        carry[...] = jnp.zeros_like(carry)

    r_i = lax.broadcasted_iota(jnp.int32, (tm, tm), 0)
    c_i = lax.broadcasted_iota(jnp.int32, (tm, tm), 1)
    earlier = jnp.where(r_i > c_i, 1.0, 0.0).astype(BF16)
    rank = _dot(earlier, chosen.astype(BF16)) + carry[...]
    rank_sel = jnp.zeros((tm, LANES), F32)
    for k in range(TOP_K):
        rk = jnp.sum(jnp.where(lane_f == picks[k], rank, 0.0), axis=1, keepdims=True)
        rank_sel = jnp.where(lane == k, rk, rank_sel)
    rank_ref[...] = rank_sel.astype(jnp.int32)
    total = carry[...] + jnp.sum(chosen, axis=0, keepdims=True)
    carry[...] = total
    cnt_ref[...] = total.astype(jnp.int32)


def router(x, w_router, r_bias, tm=256):
    n, d = x.shape
    n_exp = w_router.shape[1]
    assert n_exp <= LANES and n_exp % N_GROUPS == 0
    tm = _pick(n, tm)
    wr = jnp.pad(w_router, ((0, 0), (0, LANES - n_exp))).astype(BF16)
    rb = jnp.pad(r_bias, (0, LANES - n_exp)).reshape(1, LANES)
    row = lambda i: (i, 0)
    fix = lambda i: (0, 0)
    return pl.pallas_call(
        functools.partial(_router_kernel, n_exp=n_exp),
        grid=(n // tm,),
        in_specs=[pl.BlockSpec((tm, d), row), pl.BlockSpec((d, LANES), fix), pl.BlockSpec((1, LANES), fix)],
        out_specs=[pl.BlockSpec((tm, LANES), row), pl.BlockSpec((tm, LANES), row),
                   pl.BlockSpec((tm, LANES), row), pl.BlockSpec((1, LANES), fix)],
        out_shape=[jax.ShapeDtypeStruct((n, LANES), jnp.int32), jax.ShapeDtypeStruct((n, LANES), F32),
                   jax.ShapeDtypeStruct((n, LANES), jnp.int32), jax.ShapeDtypeStruct((1, LANES), jnp.int32)],
        scratch_shapes=[pltpu.VMEM((1, LANES), F32)],
        compiler_params=_params(("arbitrary",)),
        name="router",
    )(x, wr, rb)


def expert_plan(idx, rank, counts, n_exp, tm):
    n_tok = idx.shape[0]
    n_asg = n_tok * TOP_K
    n_tiles = n_asg // tm
    n_items = n_tiles + n_exp - 1
    counts = counts[0, :n_exp]
    ends = jnp.cumsum(counts)
    starts = ends - counts
    first_blk = starts // tm
    n_items_e = jnp.where(counts > 0, (ends - 1) // tm - first_blk + 1, 0)
    item_end = jnp.cumsum(n_items_e)
    item_start = item_end - n_items_e
    n_live = item_end[-1]

    e_ids = jnp.arange(n_exp, dtype=jnp.int32)
    mine = idx[:, :TOP_K, None] == e_ids
    lookup = lambda table: jnp.sum(jnp.where(mine, table, 0), axis=-1)
    pos = rank[:, :TOP_K] + lookup(starts)
    pos_out = (lookup(item_start - first_blk) + pos // tm) * tm + pos % tm

    it = jnp.minimum(jnp.arange(n_items, dtype=jnp.int32), n_live - 1)
    onehot = (jnp.sum(item_end[None, :] <= it[:, None], axis=1)[:, None] == e_ids).astype(jnp.int32)
    pick = lambda v: jnp.sum(onehot * v[None, :], axis=1)
    item_exp = pick(e_ids)
    item_blk = jnp.clip(pick(first_blk) + it - pick(item_start), 0, n_tiles - 1)
    new_exp = jnp.concatenate([jnp.ones((1,), jnp.int32), (item_exp[1:] != item_exp[:-1]).astype(jnp.int32)])
    tables = tuple(t.astype(jnp.int32) for t in (item_blk, item_exp, new_exp, n_live.reshape(1)))
    return pos.astype(jnp.int32), pos_out.astype(jnp.int32), tables


def _dispatch_kernel(pos_ref, x_ref, xs_hbm, sem):
    tt = x_ref.shape[0]
    for t in range(tt):
        for k in range(TOP_K):
            pltpu.make_async_copy(x_ref.at[pl.ds(t, 1)], xs_hbm.at[pl.ds(pos_ref[0, 0, t * TOP_K + k], 1)],
                                  sem).start(priority=k % 2)
    for _ in range(TOP_K):
        pltpu.make_async_copy(x_ref, xs_hbm.at[pl.ds(0, tt)], sem).wait()


def dispatch(x, pos, tt=128):
    n, d = x.shape
    tt = _pick(n, tt)
    pos3 = pos.reshape(n // tt, 1, tt * TOP_K)
    return pl.pallas_call(
        _dispatch_kernel,
        grid=(n // tt,),
        in_specs=[pl.BlockSpec((1, 1, tt * TOP_K), lambda i: (i, 0, 0), memory_space=pltpu.SMEM),
                  pl.BlockSpec((tt, d), lambda i: (i, 0))],
        out_specs=pl.BlockSpec(memory_space=pl.ANY),
        out_shape=jax.ShapeDtypeStruct((n * TOP_K, d), x.dtype),
        scratch_shapes=[pltpu.SemaphoreType.DMA],
        compiler_params=_params(("arbitrary",)),
        name="dispatch",
    )(pos3, x)


def _experts_kernel(blk_ref, exp_ref, newexp_ref, nlive_ref, xs_ref, wg_ref, wu_ref, wd_ref, y_ref, wgb, wub, wdb):
    i = pl.program_id(0)

    @pl.when(i < nlive_ref[0])
    def _():
        @pl.when(newexp_ref[i] == 1)
        def _():
            wgb[...] = wg_ref[...].astype(BF16)
            wub[...] = wu_ref[...].astype(BF16)
            wdb[...] = wd_ref[...].astype(BF16)

        x = xs_ref[...].astype(BF16)
        g = _dot(x, wgb[...])
        u = _dot(x, wub[...])
        h = (g * jax.nn.sigmoid(g) * u).astype(BF16)
        y_ref[...] = _dot(h, wdb[...])

    @pl.when(i >= nlive_ref[0])
    def _():
        y_ref[...] = jnp.zeros_like(y_ref)


def routed_experts(xs, tables, w_gate, w_up, w_down, layer, tm):
    n_asg, d = xs.shape
    n_exp, f = w_gate.shape[1], w_gate.shape[3]
    n_items = n_asg // tm + n_exp - 1
    w_in = pl.BlockSpec((None, None, d, f), lambda i, blk, ex, *_: (layer, ex[i], 0, 0))
    w_out = pl.BlockSpec((None, None, f, d), lambda i, blk, ex, *_: (layer, ex[i], 0, 0))
    grid_spec = pltpu.PrefetchScalarGridSpec(
        num_scalar_prefetch=4,
        grid=(n_items,),
        in_specs=[pl.BlockSpec((tm, d), lambda i, blk, *_: (blk[i], 0)), w_in, w_in, w_out],
        out_specs=pl.BlockSpec((tm, d), lambda i, *_: (i, 0)),
        scratch_shapes=[pltpu.VMEM((d, f), BF16), pltpu.VMEM((d, f), BF16), pltpu.VMEM((f, d), BF16)],
    )
    return pl.pallas_call(
        _experts_kernel,
        grid_spec=grid_spec,
        out_shape=jax.ShapeDtypeStruct((n_items * tm, d), F32),
        compiler_params=_params(("arbitrary",)),
        name="routed_experts",
    )(*tables, xs, w_gate, w_up, w_down)


def _combine_kernel(pos_ref, posn_ref, y_hbm, wt_ref, xb_ref, sg_ref, su_ref, sd_ref, res_ref, g_ref, b_ref,
                    o_ref, ob_ref, ybuf, sems):
    i = pl.program_id(0)
    n_steps = pl.num_programs(0)
    tc = res_ref.shape[0]

    def gather_wait(s):
        for k in range(TOP_K):
            pltpu.make_async_copy(y_hbm.at[pl.ds(0, tc)], ybuf.at[s, k], sems.at[s]).wait()

    @pl.when(i == 0)
    def _():
        def token(t, c):
            for k in range(TOP_K):
                pltpu.make_async_copy(y_hbm.at[pl.ds(pos_ref[0, 0, t * TOP_K + k], 1)],
                                      ybuf.at[0, k, pl.ds(t, 1)], sems.at[0]).start(priority=k % 2)
            return c
        lax.fori_loop(0, tc, token, 0)

    def step(slot):
        nxt = 1 - slot
        for t in range(tc):
            for k in range(TOP_K):
                pltpu.make_async_copy(y_hbm.at[pl.ds(posn_ref[0, 0, t * TOP_K + k], 1)],
                                      ybuf.at[nxt, k, pl.ds(t, 1)], sems.at[nxt]).start(priority=k % 2)

        xb = xb_ref[...]
        hg = _dot(xb, sg_ref[...])
        hu = _dot(xb, su_ref[...])
        shared = _dot((hg * jax.nn.sigmoid(hg) * hu).astype(BF16), sd_ref[...])
        base = DN_ALPHA * res_ref[...] + shared

        gather_wait(slot)
        wts = wt_ref[...]
        routed = ybuf[slot, 0] * wts[:, 0:1]
        for k in range(1, TOP_K):
            routed = routed + ybuf[slot, k] * wts[:, k:k + 1]
        z = _layer_norm(base + routed, g_ref[...], b_ref[...])
        o_ref[...] = z
        ob_ref[...] = z.astype(BF16)

        @pl.when(i == n_steps - 1)
        def _():
            gather_wait(nxt)

    for slot in range(2):
        pl.when(i % 2 == slot)(functools.partial(step, slot))


def combine_shared_ln(y, pos, wts, xb, s_gate, s_up, s_down, res, g, b, tc=128):
    n, d = res.shape
    f = s_gate.shape[1]
    tc = _pick(n, tc)
    n_steps = n // tc
    pos3 = pos.reshape(n_steps, 1, tc * TOP_K)
    row = lambda i: (i, 0)
    fix = lambda i: (0, 0)
    pos_blk = lambda imap: pl.BlockSpec((1, 1, tc * TOP_K), imap, memory_space=pltpu.SMEM)
    return pl.pallas_call(
        _combine_kernel,
        grid=(n_steps,),
        in_specs=[pos_blk(lambda i: (i, 0, 0)),
                  pos_blk(lambda i: (jnp.minimum(i + 1, n_steps - 1), 0, 0)),
                  pl.BlockSpec(memory_space=pl.ANY),
                  pl.BlockSpec((tc, LANES), row), pl.BlockSpec((tc, d), row),
                  pl.BlockSpec((d, f), fix), pl.BlockSpec((d, f), fix), pl.BlockSpec((f, d), fix),
                  pl.BlockSpec((tc, d), row), pl.BlockSpec((1, d), fix), pl.BlockSpec((1, d), fix)],
        out_specs=[pl.BlockSpec((tc, d), row), pl.BlockSpec((tc, d), row)],
        out_shape=[jax.ShapeDtypeStruct((n, d), F32), jax.ShapeDtypeStruct((n, d), BF16)],
        scratch_shapes=[pltpu.VMEM((2, TOP_K, tc, d), F32), pltpu.SemaphoreType.DMA((2,))],
        compiler_params=_params(("arbitrary",)),
        name="combine_shared_ln",
    )(pos3, pos3, y, wts, xb, s_gate, s_up, s_down, res, g.reshape(1, d), b.reshape(1, d))


def moe_ffn_ln(xf, xb, layer, w_router, r_bias, w_gate, w_up, w_down, s_gate, s_up, s_down, g, b, tm=512):
    n_exp = w_router.shape[1]
    tm = _pick(xf.shape[0] * TOP_K, tm)
    idx, wts, rank, counts = router(xb, w_router, r_bias)
    pos, pos_out, tables = expert_plan(idx, rank, counts, n_exp, tm)
    xs = dispatch(xf, pos)
    y = routed_experts(xs, tables, w_gate, w_up, w_down, layer, tm)
    return combine_shared_ln(y, pos_out, wts, xb, s_gate.astype(BF16), s_up.astype(BF16), s_down.astype(BF16),
                             xf, g, b)


def kernel(x, rel_bias, attn_w_qkv, attn_w_o, conv_w_in, conv_b_in, conv_w_dw, conv_b_dw,
           conv_ln_g, conv_ln_b, conv_w_out, conv_b_out, ln_mix_g, ln_mix_b, ln_ffn_g, ln_ffn_b,
           moe_w_router, moe_router_bias, moe_w_gate, moe_w_up, moe_w_down,
           shared_w_gate, shared_w_up, shared_w_down):
    batch, seq, d = x.shape
    n = batch * seq
    depth = ln_mix_g.shape[0]
    nb = seq // MOBA_BLOCK
    assert seq % MOBA_BLOCK == 0 and d % N_HEADS == 0 and depth == DEPTH

    tiles = bias_tiles(rel_bias, N_HEADS, _num_bias_tiles(nb), (d // N_HEADS) ** -0.5)
    xf = x.reshape(n, d)
    xb = xf.astype(BF16)
    zero_bias = jnp.zeros((d,), F32)
    for i in range(depth):
        m = i // 2
        if i % 2 == 0:
            qkv = matmul(xb, attn_w_qkv[m].astype(BF16), BF16)
            a = moba_attention(qkv, tiles, batch, seq)
            xf, xb = proj_res_ln(a, attn_w_o[m].astype(BF16), zero_bias, xf, ln_mix_g[i], ln_mix_b[i])
        else:
            h = glu_proj(xb, conv_w_in[m].astype(BF16), conv_b_in[m])
            a = dwconv_ln_silu(h, conv_w_dw[m], conv_b_dw[m], conv_ln_g[m], conv_ln_b[m], batch, seq)
            xf, xb = proj_res_ln(a, conv_w_out[m].astype(BF16), conv_b_out[m], xf, ln_mix_g[i], ln_mix_b[i])
        xf, xb = moe_ffn_ln(xf, xb, i, moe_w_router[i], moe_router_bias[i], moe_w_gate, moe_w_up, moe_w_down,
                            shared_w_gate[i], shared_w_up[i], shared_w_down[i], ln_ffn_g[i], ln_ffn_b[i])
    return xf.reshape(batch, seq, d)
```

```python
import functools
import math

import jax
import jax.numpy as jnp
from jax import lax
from jax.experimental import pallas as pl
from jax.experimental.pallas import tpu as pltpu

N_HEADS = 16
MOBA_BLOCK = 256
MOBA_TOPK = 3
REL_BUCKETS = 32
REL_MAX_DIST = 2048
CONV_K = 31
TOP_K = 8
N_GROUPS = 8
TOPK_GROUPS = 4
ROUTED_SCALE = 2.5
DEPTH = 4
DN_ALPHA = (2 * DEPTH) ** 0.25
LN_EPS = 1e-5

LANES = 128
SUBLANES = 8
VMEM_LIMIT_BYTES = 56 * 1024 * 1024

NEG_BIG = -1e30
KV_STEP = 4
HEADS_STEP = 4

F32 = jnp.float32
BF16 = jnp.bfloat16


def _params(sem):
    return pltpu.CompilerParams(dimension_semantics=sem, vmem_limit_bytes=VMEM_LIMIT_BYTES)


def _pick(n, pref):
    t = min(pref, n)
    while n % t:
        t //= 2
    return t


def _layer_norm(z, g, b):
    mu = jnp.mean(z, axis=-1, keepdims=True)
    zc = z - mu
    var = jnp.mean(zc * zc, axis=-1, keepdims=True)
    return zc * lax.rsqrt(var + LN_EPS) * g + b


def _dot(a, b):
    return jnp.dot(a, b, preferred_element_type=F32)


def _dot_nt(a, b):
    return lax.dot_general(a, b, (((1,), (1,)), ((), ())), preferred_element_type=F32)


def _mm_kernel(x_ref, w_ref, o_ref):
    o_ref[...] = _dot(x_ref[...], w_ref[...]).astype(o_ref.dtype)


def matmul(x, w, out_dtype, tm=1024, tn=512):
    n, k = x.shape
    m = w.shape[1]
    tm, tn = _pick(n, tm), _pick(m, tn)
    return pl.pallas_call(
        _mm_kernel,
        grid=(n // tm, m // tn),
        in_specs=[pl.BlockSpec((tm, k), lambda i, j: (i, 0)),
                  pl.BlockSpec((k, tn), lambda i, j: (0, j))],
        out_specs=pl.BlockSpec((tm, tn), lambda i, j: (i, j)),
        out_shape=jax.ShapeDtypeStruct((n, m), out_dtype),
        compiler_params=_params(("parallel", "arbitrary")),
        name="matmul",
    )(x, w)


def _glu_kernel(x_ref, wa_ref, wg_ref, ba_ref, bg_ref, o_ref):
    x = x_ref[...]
    a = _dot(x, wa_ref[...]) + ba_ref[...]
    g = _dot(x, wg_ref[...]) + bg_ref[...]
    o_ref[...] = a * jax.nn.sigmoid(g)


def glu_proj(x, w, b, tm=1024, tn=256):
    n, k = x.shape
    d = w.shape[1] // 2
    tm, tn = _pick(n, tm), _pick(d, tn)
    nj = d // tn
    b2 = b.reshape(1, 2 * d)
    return pl.pallas_call(
        _glu_kernel,
        grid=(n // tm, nj),
        in_specs=[pl.BlockSpec((tm, k), lambda i, j: (i, 0)),
                  pl.BlockSpec((k, tn), lambda i, j: (0, j)),
                  pl.BlockSpec((k, tn), lambda i, j: (0, j + nj)),
                  pl.BlockSpec((1, tn), lambda i, j: (0, j)),
                  pl.BlockSpec((1, tn), lambda i, j: (0, j + nj))],
        out_specs=pl.BlockSpec((tm, tn), lambda i, j: (i, j)),
        out_shape=jax.ShapeDtypeStruct((n, d), F32),
        compiler_params=_params(("parallel", "arbitrary")),
        name="glu_proj",
    )(x, w, w, b2, b2)


def _proj_res_ln_kernel(a_ref, w_ref, bias_ref, res_ref, g_ref, b_ref, o_ref, ob_ref):
    y = _dot(a_ref[...], w_ref[...]) + bias_ref[...]
    z = _layer_norm(DN_ALPHA * res_ref[...] + y, g_ref[...], b_ref[...])
    o_ref[...] = z
    ob_ref[...] = z.astype(BF16)


def proj_res_ln(a, w, bias, res, g, b, tm=256):
    n, k = a.shape
    d = w.shape[1]
    tm = _pick(n, tm)
    row = lambda i: (i, 0)
    fix = lambda i: (0, 0)
    return pl.pallas_call(
        _proj_res_ln_kernel,
        grid=(n // tm,),
        in_specs=[pl.BlockSpec((tm, k), row), pl.BlockSpec((k, d), fix),
                  pl.BlockSpec((1, d), fix), pl.BlockSpec((tm, d), row),
                  pl.BlockSpec((1, d), fix), pl.BlockSpec((1, d), fix)],
        out_specs=[pl.BlockSpec((tm, d), row), pl.BlockSpec((tm, d), row)],
        out_shape=[jax.ShapeDtypeStruct((n, d), F32), jax.ShapeDtypeStruct((n, d), BF16)],
        compiler_params=_params(("parallel",)),
        name="proj_res_ln",
    )(a, w, bias.reshape(1, d), res, g.reshape(1, d), b.reshape(1, d))


def _t5_bucket(rel):
    n = jnp.maximum(rel, 0)
    max_exact = REL_BUCKETS // 2
    nf = jnp.maximum(n, 1).astype(F32)
    large = max_exact + (jnp.log(nf / max_exact) / math.log(REL_MAX_DIST / max_exact)
                         * (REL_BUCKETS - max_exact)).astype(jnp.int32)
    large = jnp.minimum(large, REL_BUCKETS - 1)
    return jnp.where(n < max_exact, n, large)


def _bias_tile_kernel(rb_ref, o_ref, *, inv_scale):
    h = pl.program_id(0)
    d = pl.program_id(1)
    blk = o_ref.shape[-1]
    row = lax.broadcasted_iota(jnp.int32, (blk, blk), 0)
    col = lax.broadcasted_iota(jnp.int32, (blk, blk), 1)
    rel = d * blk + row - col
    bucket = _t5_bucket(rel)
    acc = jnp.zeros((blk, blk), F32)
    for k in range(REL_BUCKETS):
        acc = jnp.where(bucket == k, rb_ref[k, h], acc)
    o_ref[0, 0] = jnp.where(rel >= 0, acc * inv_scale, NEG_BIG)


def _num_bias_tiles(nb):
    last_start = (REL_MAX_DIST / (REL_BUCKETS // 2)) ** ((REL_BUCKETS // 2 - 1) / (REL_BUCKETS // 2)) \
        * (REL_BUCKETS // 2)
    d = 1
    while (d - 1) * MOBA_BLOCK + 1 < 1.05 * last_start + 1:
        d += 1
    return min(nb, d + 1)


def bias_tiles(rel_bias, n_heads, n_tiles, scale):
    return pl.pallas_call(
        functools.partial(_bias_tile_kernel, inv_scale=1.0 / scale),
        grid=(n_heads, n_tiles),
        in_specs=[pl.BlockSpec(memory_space=pltpu.SMEM)],
        out_specs=pl.BlockSpec((1, 1, MOBA_BLOCK, MOBA_BLOCK), lambda h, d: (h, d, 0, 0)),
        out_shape=jax.ShapeDtypeStruct((n_heads, n_tiles, MOBA_BLOCK, MOBA_BLOCK), F32),
        compiler_params=_params(("parallel", "parallel")),
        name="bias_tiles",
    )(rel_bias)


def _moba_kernel(q_ref, k_ref, v_ref, bias_ref, o_ref, kmean_sc, kaug_sc, *, nb, n_bias, scale):
    j = pl.program_id(2)
    blk = MOBA_BLOCK
    dh = q_ref.shape[2] // HEADS_STEP
    seq = k_ref.shape[1]
    c_exp = scale * math.log2(math.e)
    cw = KV_STEP * blk

    @pl.when(j == 0)
    def _():
        kmean_sc[...] = jnp.zeros_like(kmean_sc)
        for n in range(nb):
            kn = k_ref[0, n * blk:(n + 1) * blk, :].astype(F32)
            kmean_sc[n:n + 1, :] = jnp.mean(kn, axis=0, keepdims=True)
        blk_of_row = lax.broadcasted_iota(jnp.int32, (seq, LANES), 0) // blk
        blk_lane = lax.broadcasted_iota(jnp.int32, (seq, LANES), 1)
        onehot = jnp.where(blk_of_row == blk_lane, 1.0, 0.0).astype(BF16)
        for h in range(HEADS_STEP):
            kaug_sc[h, :, 0:dh] = k_ref[0, :, h * dh:(h + 1) * dh]
            kaug_sc[h, :, dh:dh + LANES] = onehot

    lane = lax.broadcasted_iota(jnp.int32, (blk, LANES), 1)
    lane_f = lane.astype(F32)
    j0 = pl.multiple_of(j * blk, blk)

    q_aug, state = [], []
    for h in range(HEADS_STEP):
        q = q_ref[0, :, h * dh:(h + 1) * dh]
        gate = _dot_nt(q, kmean_sc[:, h * dh:(h + 1) * dh].astype(BF16))
        gate = jnp.where(lane < j, gate, -jnp.inf)
        sel = jnp.zeros((blk, LANES), F32)
        for _ in range(MOBA_TOPK):
            mx = jnp.max(gate, axis=1, keepdims=True)
            first = jnp.min(jnp.where(gate == mx, lane_f, float(LANES)), axis=1, keepdims=True)
            hit = lane_f == first
            sel = jnp.where(hit, jnp.where(mx > -jnp.inf, 1.0, sel), sel)
            gate = jnp.where(hit, -jnp.inf, gate)
        selneg = jnp.where(sel > 0.0, 0.0, NEG_BIG).astype(BF16)
        q_aug.append(jnp.concatenate([q, selneg], axis=1))

        t = _dot_nt(q, k_ref[0, pl.ds(j0, blk), h * dh:(h + 1) * dh]) + bias_ref[h, 0]
        m0 = jnp.max(t, axis=1, keepdims=True)
        p = jnp.exp2((t - m0) * c_exp)
        l0 = jnp.sum(p, axis=1, keepdims=True)
        acc0 = _dot(p.astype(BF16), v_ref[0, pl.ds(j0, blk), h * dh:(h + 1) * dh])
        state.append((m0, l0, acc0))

    def body(c, carry):
        c0 = pl.multiple_of(c * cw, cw)
        out = []
        for h in range(HEADS_STEP):
            m, l, acc = carry[h]
            bias = jnp.concatenate(
                [bias_ref[h, jnp.clip(j - (c * KV_STEP + u), 0, n_bias - 1)] for u in range(KV_STEP)], axis=1)
            t = _dot_nt(q_aug[h], kaug_sc[h, pl.ds(c0, cw), :]) + bias
            m_new = jnp.maximum(m, jnp.max(t, axis=1, keepdims=True))
            alpha = jnp.exp2((m - m_new) * c_exp)
            p = jnp.exp2((t - m_new) * c_exp)
            l = alpha * l + jnp.sum(p, axis=1, keepdims=True)
            acc = alpha * acc + _dot(p.astype(BF16), v_ref[0, pl.ds(c0, cw), h * dh:(h + 1) * dh])
            out.append((m_new, l, acc))
        return tuple(out)

    state = lax.fori_loop(0, (j + KV_STEP - 1) // KV_STEP, body, tuple(state))
    o_ref[0] = jnp.concatenate([acc / l for (_, l, acc) in state], axis=1).astype(o_ref.dtype)


def moba_attention(qkv, tiles, batch, seq):
    d3 = qkv.shape[1]
    d = d3 // 3
    dh = d // N_HEADS
    nb = seq // MOBA_BLOCK
    assert nb % KV_STEP == 0 and N_HEADS % HEADS_STEP == 0 and dh == LANES
    n_bias = tiles.shape[1]
    hp = N_HEADS // HEADS_STEP
    wh = HEADS_STEP * dh
    qkv3 = qkv.reshape(batch, seq, d3)
    kern = functools.partial(_moba_kernel, nb=nb, n_bias=n_bias, scale=dh ** -0.5)
    out = pl.pallas_call(
        kern,
        grid=(batch, hp, nb),
        in_specs=[pl.BlockSpec((1, MOBA_BLOCK, wh), lambda b, h, j: (b, j, h)),
                  pl.BlockSpec((1, seq, wh), lambda b, h, j: (b, 0, hp + h)),
                  pl.BlockSpec((1, seq, wh), lambda b, h, j: (b, 0, 2 * hp + h)),
                  pl.BlockSpec((HEADS_STEP, n_bias, MOBA_BLOCK, MOBA_BLOCK), lambda b, h, j: (h, 0, 0, 0))],
        out_specs=pl.BlockSpec((1, MOBA_BLOCK, wh), lambda b, h, j: (b, j, h)),
        out_shape=jax.ShapeDtypeStruct((batch, seq, d), BF16),
        scratch_shapes=[pltpu.VMEM((LANES, wh), F32), pltpu.VMEM((HEADS_STEP, seq, dh + LANES), BF16)],
        compiler_params=_params(("parallel", "parallel", "arbitrary")),
        name="moba_attention",
    )(qkv3, qkv3, qkv3, tiles)
    return out.reshape(batch * seq, d)


def _dwconv_ln_kernel(prev_ref, cur_ref, w_ref, bdw_ref, g_ref, b_ref, o_ref, buf, conv_sc, hp_sc, *, halo):
    i = pl.program_id(1)
    ts, d = cur_ref.shape[1], cur_ref.shape[2]
    prev = prev_ref[0]
    buf[0:halo, :] = jnp.where(i > 0, prev, jnp.zeros_like(prev))
    buf[halo:halo + ts, :] = cur_ref[0]
    first = halo - (CONV_K - 1)
    for c in range(d // LANES):
        cs = slice(c * LANES, (c + 1) * LANES)
        acc = jnp.zeros((ts, LANES), F32) + bdw_ref[:, cs]
        for p in range(SUBLANES):
            span = (ts + halo - p) // SUBLANES * SUBLANES
            hp_sc[0:span, :] = buf[p:p + span, cs]
            for a in range(span // SUBLANES):
                k = a * SUBLANES + p - first
                if 0 <= k < CONV_K and a * SUBLANES + ts <= span:
                    acc = acc + hp_sc[a * SUBLANES:a * SUBLANES + ts, :] * w_ref[k:k + 1, cs]
        conv_sc[:, cs] = acc
    y = _layer_norm(conv_sc[...], g_ref[...], b_ref[...])
    o_ref[0] = (y * jax.nn.sigmoid(y)).astype(o_ref.dtype)


def dwconv_ln_silu(h, w_dw, b_dw, g, b, batch, seq, ts=256):
    d = h.shape[1]
    ts = _pick(seq, ts)
    halo = 32
    assert halo >= CONV_K - 1 and ts % halo == 0 and d % LANES == 0
    r = ts // halo
    h3 = h.reshape(batch, seq, d)
    fix = lambda bi, i: (0, 0)
    kern = functools.partial(_dwconv_ln_kernel, halo=halo)
    out = pl.pallas_call(
        kern,
        grid=(batch, seq // ts),
        in_specs=[pl.BlockSpec((1, halo, d), lambda bi, i: (bi, jnp.maximum(i * r - 1, 0), 0)),
                  pl.BlockSpec((1, ts, d), lambda bi, i: (bi, i, 0)),
                  pl.BlockSpec((CONV_K, d), fix), pl.BlockSpec((1, d), fix),
                  pl.BlockSpec((1, d), fix), pl.BlockSpec((1, d), fix)],
        out_specs=pl.BlockSpec((1, ts, d), lambda bi, i: (bi, i, 0)),
        out_shape=jax.ShapeDtypeStruct((batch, seq, d), BF16),
        scratch_shapes=[pltpu.VMEM((halo + ts, d), F32), pltpu.VMEM((ts, d), F32),
                        pltpu.VMEM((halo + ts, LANES), F32)],
        compiler_params=_params(("parallel", "parallel")),
        name="dwconv_ln_silu",
    )(h3, h3, w_dw, b_dw.reshape(1, d), g.reshape(1, d), b.reshape(1, d))
    return out.reshape(batch * seq, d)


def _router_kernel(x_ref, wr_ref, rb_ref, idx_ref, wt_ref, rank_ref, cnt_ref, carry, *, n_exp):
    step = pl.program_id(0)
    tm = x_ref.shape[0]
    per_g = n_exp // N_GROUPS
    assert per_g == SUBLANES
    logits = _dot_nt(wr_ref[...], x_ref[...])
    scores = jax.nn.sigmoid(logits)
    biased = scores + rb_ref[...]
    row_f = lax.broadcasted_iota(jnp.int32, (n_exp, tm), 0).astype(F32)

    def first_argmax(v, ids, none):
        mx = jnp.max(v, axis=0, keepdims=True)
        first = jnp.min(jnp.where(v == mx, ids, none), axis=0, keepdims=True)
        return mx, first

    grp_f = (lax.broadcasted_iota(jnp.int32, (n_exp, tm), 0) // per_g).astype(F32)
    gs = jnp.full((n_exp, tm), -jnp.inf, F32)
    for g in range(N_GROUPS):
        in_g = grp_f == float(g)
        vg = jnp.where(in_g, biased, -jnp.inf)
        m1, f1 = first_argmax(vg, row_f, float(n_exp))
        rest = jnp.where(row_f == f1, -jnp.inf, vg)
        m2 = jnp.max(rest, axis=0, keepdims=True)
        gs = jnp.where(in_g, jnp.maximum(vg, m1) + jnp.maximum(rest, m2), gs)
    keep = jnp.zeros((n_exp, tm), F32)
    for _ in range(TOPK_GROUPS):
        _, fg = first_argmax(gs, grp_f, float(N_GROUPS))
        hit = grp_f == fg
        keep = jnp.where(hit, 1.0, keep)
        gs = jnp.where(hit, -jnp.inf, gs)
    cand = jnp.where(keep > 0.0, biased, -jnp.inf)
    chosen = jnp.zeros((n_exp, tm), F32)
    picks, wks = [], []
    for k in range(TOP_K):
        _, fe = first_argmax(cand, row_f, float(n_exp))
        hit = row_f == fe
        wks.append(jnp.sum(jnp.where(hit, scores, 0.0), axis=0, keepdims=True))
        chosen = jnp.where(hit, 1.0, chosen)
        cand = jnp.where(hit, -jnp.inf, cand)
        picks.append(fe)
    wsum = wks[0]
    for k in range(1, TOP_K):
        wsum = wsum + wks[k]

    @pl.when(step == 0)
    def _():
        carry[...] = jnp.zeros_like(carry)

    r_i = lax.broadcasted_iota(jnp.int32, (tm, tm), 0)
    c_i = lax.broadcasted_iota(jnp.int32, (tm, tm), 1)
    earlier = jnp.where(r_i < c_i, 1.0, 0.0).astype(BF16)
    rank = _dot(chosen.astype(BF16), earlier) + carry[...]
    rks = [jnp.sum(jnp.where(row_f == picks[k], rank, 0.0), axis=0, keepdims=True) for k in range(TOP_K)]
    idx_ref[...] = jnp.concatenate(picks, axis=0).astype(jnp.int32)
    wt_ref[...] = jnp.concatenate(wks, axis=0) / wsum * ROUTED_SCALE
    rank_ref[...] = jnp.concatenate(rks, axis=0).astype(jnp.int32)
    total = carry[...] + jnp.broadcast_to(jnp.sum(chosen, axis=1, keepdims=True), chosen.shape)
    carry[...] = total
    cnt_ref[...] = total.astype(jnp.int32)


def router(x, w_router, r_bias, tm=256):
    n, d = x.shape
    n_exp = w_router.shape[1]
    assert n_exp <= LANES and n_exp % N_GROUPS == 0
    tm = _pick(n, tm)
    wr_t = w_router.T.astype(BF16)
    rb = jnp.broadcast_to(r_bias.astype(F32)[:, None], (n_exp, tm))
    col = lambda i: (0, i)
    fix = lambda i: (0, 0)
    idx_t, wts_t, rank_t, counts = pl.pallas_call(
        functools.partial(_router_kernel, n_exp=n_exp),
        grid=(n // tm,),
        in_specs=[pl.BlockSpec((tm, d), lambda i: (i, 0)), pl.BlockSpec((n_exp, d), fix),
                  pl.BlockSpec((n_exp, tm), fix)],
        out_specs=[pl.BlockSpec((TOP_K, tm), col), pl.BlockSpec((TOP_K, tm), col),
                   pl.BlockSpec((TOP_K, tm), col), pl.BlockSpec((n_exp, tm), fix)],
        out_shape=[jax.ShapeDtypeStruct((TOP_K, n), jnp.int32), jax.ShapeDtypeStruct((TOP_K, n), F32),
                   jax.ShapeDtypeStruct((TOP_K, n), jnp.int32), jax.ShapeDtypeStruct((n_exp, tm), jnp.int32)],
        scratch_shapes=[pltpu.VMEM((n_exp, tm), F32)],
        compiler_params=_params(("arbitrary",)),
        name="router",
    )(x, wr_t, rb)
    return idx_t.T, wts_t.T, rank_t.T, counts[:, 0]


def expert_plan(idx, rank, counts, n_exp, tm):
    n_tok = idx.shape[0]
    n_asg = n_tok * TOP_K
    n_tiles = n_asg // tm
    n_items = n_tiles + n_exp - 1
    ends = jnp.cumsum(counts)
    starts = ends - counts
    first_blk = starts // tm
    n_items_e = jnp.where(counts > 0, (ends - 1) // tm - first_blk + 1, 0)
    item_end = jnp.cumsum(n_items_e)
    item_start = item_end - n_items_e
    n_live = item_end[-1]

    e_ids = jnp.arange(n_exp, dtype=jnp.int32)
    mine = idx[:, :, None] == e_ids
    lookup = lambda table: jnp.sum(jnp.where(mine, table, 0), axis=-1)
    pos = rank + lookup(starts)
    pos_out = (lookup(item_start - first_blk) + pos // tm) * tm + pos % tm

    it = jnp.minimum(jnp.arange(n_items, dtype=jnp.int32), n_live - 1)
    onehot = (jnp.sum(item_end[None, :] <= it[:, None], axis=1)[:, None] == e_ids).astype(jnp.int32)
    pick = lambda v: jnp.sum(onehot * v[None, :], axis=1)
    item_exp = pick(e_ids)
    item_blk = jnp.clip(pick(first_blk) + it - pick(item_start), 0, n_tiles - 1)
    new_exp = jnp.concatenate([jnp.ones((1,), jnp.int32), (item_exp[1:] != item_exp[:-1]).astype(jnp.int32)])
    tables = tuple(t.astype(jnp.int32) for t in (item_blk, item_exp, new_exp, n_live.reshape(1)))
    return pos.astype(jnp.int32), pos_out.astype(jnp.int32), tables


def _dispatch_kernel(pos_ref, x_ref, xs_hbm, sem):
    tt = x_ref.shape[0]
    for t in range(tt):
        for k in range(TOP_K):
            pltpu.make_async_copy(x_ref.at[pl.ds(t, 1)], xs_hbm.at[pl.ds(pos_ref[0, 0, t * TOP_K + k], 1)],
                                  sem).start(priority=k % 2)
    for _ in range(TOP_K):
        pltpu.make_async_copy(x_ref, xs_hbm.at[pl.ds(0, tt)], sem).wait()


def dispatch(x, pos, tt=128):
    n, d = x.shape
    tt = _pick(n, tt)
    pos3 = pos.reshape(n // tt, 1, tt * TOP_K)
    return pl.pallas_call(
        _dispatch_kernel,
        grid=(n // tt,),
        in_specs=[pl.BlockSpec((1, 1, tt * TOP_K), lambda i: (i, 0, 0), memory_space=pltpu.SMEM),
                  pl.BlockSpec((tt, d), lambda i: (i, 0))],
        out_specs=pl.BlockSpec(memory_space=pl.ANY),
        out_shape=jax.ShapeDtypeStruct((n * TOP_K, d), x.dtype),
        scratch_shapes=[pltpu.SemaphoreType.DMA],
        compiler_params=_params(("arbitrary",)),
        name="dispatch",
    )(pos3, x)


def _experts_kernel(blk_ref, exp_ref, newexp_ref, nlive_ref, xs_ref, wg_ref, wu_ref, wd_ref, y_ref, wgb, wub, wdb):
    i = pl.program_id(0)

    @pl.when(i < nlive_ref[0])
    def _():
        @pl.when(newexp_ref[i] == 1)
        def _():
            wgb[...] = wg_ref[...].astype(BF16)
            wub[...] = wu_ref[...].astype(BF16)
            wdb[...] = wd_ref[...].astype(BF16)

        x = xs_ref[...].astype(BF16)
        g = _dot(x, wgb[...])
        u = _dot(x, wub[...])
        h = (g * jax.nn.sigmoid(g) * u).astype(BF16)
        y_ref[...] = _dot(h, wdb[...])

    @pl.when(i >= nlive_ref[0])
    def _():
        y_ref[...] = jnp.zeros_like(y_ref)


def routed_experts(xs, tables, w_gate, w_up, w_down, layer, tm):
    n_asg, d = xs.shape
    n_exp, f = w_gate.shape[1], w_gate.shape[3]
    n_items = n_asg // tm + n_exp - 1
    w_in = pl.BlockSpec((None, None, d, f), lambda i, blk, ex, *_: (layer, ex[i], 0, 0))
    w_out = pl.BlockSpec((None, None, f, d), lambda i, blk, ex, *_: (layer, ex[i], 0, 0))
    grid_spec = pltpu.PrefetchScalarGridSpec(
        num_scalar_prefetch=4,
        grid=(n_items,),
        in_specs=[pl.BlockSpec((tm, d), lambda i, blk, *_: (blk[i], 0)), w_in, w_in, w_out],
        out_specs=pl.BlockSpec((tm, d), lambda i, *_: (i, 0)),
        scratch_shapes=[pltpu.VMEM((d, f), BF16), pltpu.VMEM((d, f), BF16), pltpu.VMEM((f, d), BF16)],
    )
    return pl.pallas_call(
        _experts_kernel,
        grid_spec=grid_spec,
        out_shape=jax.ShapeDtypeStruct((n_items * tm, d), F32),
        compiler_params=_params(("arbitrary",)),
        name="routed_experts",
    )(*tables, xs, w_gate, w_up, w_down)


def _combine_kernel(pos_ref, posn_ref, y_hbm, wt_ref, xb_ref, sg_ref, su_ref, sd_ref, res_ref, g_ref, b_ref,
                    o_ref, ob_ref, ybuf, sems):
    i = pl.program_id(0)
    n_steps = pl.num_programs(0)
    tc = res_ref.shape[0]

    def gather_wait(s):
        for k in range(TOP_K):
            pltpu.make_async_copy(y_hbm.at[pl.ds(0, tc)], ybuf.at[s, k], sems.at[s]).wait()

    @pl.when(i == 0)
    def _():
        def token(t, c):
            for k in range(TOP_K):
                pltpu.make_async_copy(y_hbm.at[pl.ds(pos_ref[0, 0, t * TOP_K + k], 1)],
                                      ybuf.at[0, k, pl.ds(t, 1)], sems.at[0]).start(priority=k % 2)
            return c
        lax.fori_loop(0, tc, token, 0)

    def step(slot):
        nxt = 1 - slot
        for t in range(tc):
            for k in range(TOP_K):
                pltpu.make_async_copy(y_hbm.at[pl.ds(posn_ref[0, 0, t * TOP_K + k], 1)],
                                      ybuf.at[nxt, k, pl.ds(t, 1)], sems.at[nxt]).start(priority=k % 2)

        xb = xb_ref[...]
        hg = _dot(xb, sg_ref[...])
        hu = _dot(xb, su_ref[...])
        shared = _dot((hg * jax.nn.sigmoid(hg) * hu).astype(BF16), sd_ref[...])
        base = DN_ALPHA * res_ref[...] + shared

        gather_wait(slot)
        wts = wt_ref[...]
        routed = ybuf[slot, 0] * wts[:, 0:1]
        for k in range(1, TOP_K):
            routed = routed + ybuf[slot, k] * wts[:, k:k + 1]
        z = _layer_norm(base + routed, g_ref[...], b_ref[...])
        o_ref[...] = z
        ob_ref[...] = z.astype(BF16)

        @pl.when(i == n_steps - 1)
        def _():
            gather_wait(nxt)

    for slot in range(2):
        pl.when(i % 2 == slot)(functools.partial(step, slot))


def combine_shared_ln(y, pos, wts, xb, s_gate, s_up, s_down, res, g, b, tc=128):
    n, d = res.shape
    f = s_gate.shape[1]
    tc = _pick(n, tc)
    n_steps = n // tc
    pos3 = pos.reshape(n_steps, 1, tc * TOP_K)
    row = lambda i: (i, 0)
    fix = lambda i: (0, 0)
    pos_blk = lambda imap: pl.BlockSpec((1, 1, tc * TOP_K), imap, memory_space=pltpu.SMEM)
    return pl.pallas_call(
        _combine_kernel,
        grid=(n_steps,),
        in_specs=[pos_blk(lambda i: (i, 0, 0)),
                  pos_blk(lambda i: (jnp.minimum(i + 1, n_steps - 1), 0, 0)),
                  pl.BlockSpec(memory_space=pl.ANY),
                  pl.BlockSpec((tc, LANES), row), pl.BlockSpec((tc, d), row),
                  pl.BlockSpec((d, f), fix), pl.BlockSpec((d, f), fix), pl.BlockSpec((f, d), fix),
                  pl.BlockSpec((tc, d), row), pl.BlockSpec((1, d), fix), pl.BlockSpec((1, d), fix)],
        out_specs=[pl.BlockSpec((tc, d), row), pl.BlockSpec((tc, d), row)],
        out_shape=[jax.ShapeDtypeStruct((n, d), F32), jax.ShapeDtypeStruct((n, d), BF16)],
        scratch_shapes=[pltpu.VMEM((2, TOP_K, tc, d), F32), pltpu.SemaphoreType.DMA((2,))],
        compiler_params=_params(("arbitrary",)),
        name="combine_shared_ln",
    )(pos3, pos3, y, wts, xb, s_gate, s_up, s_down, res, g.reshape(1, d), b.reshape(1, d))


def moe_ffn_ln(xf, xb, layer, w_router, r_bias, w_gate, w_up, w_down, s_gate, s_up, s_down, g, b, tm=512):
    n_exp = w_router.shape[1]
    tm = _pick(xf.shape[0] * TOP_K, tm)
    idx, wts, rank, counts = router(xb, w_router, r_bias)
    pos, pos_out, tables = expert_plan(idx, rank, counts, n_exp, tm)
    xs = dispatch(xf, pos)
    y = routed_experts(xs, tables, w_gate, w_up, w_down, layer, tm)
    return combine_shared_ln(y, pos_out, jnp.pad(wts, ((0, 0), (0, LANES - TOP_K))), xb, s_gate.astype(BF16), s_up.astype(BF16), s_down.astype(BF16),
                             xf, g, b)


def kernel(x, rel_bias, attn_w_qkv, attn_w_o, conv_w_in, conv_b_in, conv_w_dw, conv_b_dw,
           conv_ln_g, conv_ln_b, conv_w_out, conv_b_out, ln_mix_g, ln_mix_b, ln_ffn_g, ln_ffn_b,
           moe_w_router, moe_router_bias, moe_w_gate, moe_w_up, moe_w_down,
           shared_w_gate, shared_w_up, shared_w_down):
    batch, seq, d = x.shape
    n = batch * seq
    depth = ln_mix_g.shape[0]
    nb = seq // MOBA_BLOCK
    assert seq % MOBA_BLOCK == 0 and d % N_HEADS == 0 and depth == DEPTH

    tiles = bias_tiles(rel_bias, N_HEADS, _num_bias_tiles(nb), (d // N_HEADS) ** -0.5)
    xf = x.reshape(n, d)
    xb = xf.astype(BF16)
    zero_bias = jnp.zeros((d,), F32)
    for i in range(depth):
        m = i // 2
        if i % 2 == 0:
            qkv = matmul(xb, attn_w_qkv[m].astype(BF16), BF16)
            a = moba_attention(qkv, tiles, batch, seq)
            xf, xb = proj_res_ln(a, attn_w_o[m].astype(BF16), zero_bias, xf, ln_mix_g[i], ln_mix_b[i])
        else:
            h = glu_proj(xb, conv_w_in[m].astype(BF16), conv_b_in[m])
            a = dwconv_ln_silu(h, conv_w_dw[m], conv_b_dw[m], conv_ln_g[m], conv_ln_b[m], batch, seq)
            xf, xb = proj_res_ln(a, conv_w_out[m].astype(BF16), conv_b_out[m], xf, ln_mix_g[i], ln_mix_b[i])
        xf, xb = moe_ffn_ln(xf, xb, i, moe_w_router[i], moe_router_bias[i], moe_w_gate, moe_w_up, moe_w_down,
                            shared_w_gate[i], shared_w_up[i], shared_w_down[i], ln_ffn_g[i], ln_ffn_b[i])
    return xf.reshape(batch, seq, d)
```

```python
import functools
import math

import jax
import jax.numpy as jnp
from jax import lax
from jax.experimental import pallas as pl
from jax.experimental.pallas import tpu as pltpu

N_HEADS = 16
MOBA_BLOCK = 256
MOBA_TOPK = 3
REL_BUCKETS = 32
REL_MAX_DIST = 2048
CONV_K = 31
TOP_K = 8
N_GROUPS = 8
TOPK_GROUPS = 4
ROUTED_SCALE = 2.5
DEPTH = 4
DN_ALPHA = (2 * DEPTH) ** 0.25
LN_EPS = 1e-5

LANES = 128
SUBLANES = 8
VMEM_LIMIT_BYTES = 56 * 1024 * 1024

NEG_BIG = -1e30
KV_STEP = 4
HEADS_STEP = 4

F32 = jnp.float32
BF16 = jnp.bfloat16


def _params(sem):
    return pltpu.CompilerParams(dimension_semantics=sem, vmem_limit_bytes=VMEM_LIMIT_BYTES)


def _pick(n, pref):
    t = min(pref, n)
    while n % t:
        t //= 2
    return t


def _layer_norm(z, g, b):
    mu = jnp.mean(z, axis=-1, keepdims=True)
    zc = z - mu
    var = jnp.mean(zc * zc, axis=-1, keepdims=True)
    return zc * lax.rsqrt(var + LN_EPS) * g + b


def _dot(a, b):
    return jnp.dot(a, b, preferred_element_type=F32)


def _dot_nt(a, b):
    return lax.dot_general(a, b, (((1,), (1,)), ((), ())), preferred_element_type=F32)


def _mm_kernel(x_ref, w_ref, o_ref):
    o_ref[...] = _dot(x_ref[...], w_ref[...]).astype(o_ref.dtype)


def matmul(x, w, out_dtype, tm=1024, tn=1024):
    n, k = x.shape
    m = w.shape[1]
    tm, tn = _pick(n, tm), _pick(m, tn)
    return pl.pallas_call(
        _mm_kernel,
        grid=(n // tm, m // tn),
        in_specs=[pl.BlockSpec((tm, k), lambda i, j: (i, 0)),
                  pl.BlockSpec((k, tn), lambda i, j: (0, j))],
        out_specs=pl.BlockSpec((tm, tn), lambda i, j: (i, j)),
        out_shape=jax.ShapeDtypeStruct((n, m), out_dtype),
        compiler_params=_params(("parallel", "arbitrary")),
        name="matmul",
    )(x, w)


def _glu_kernel(x_ref, wa_ref, wg_ref, ba_ref, bg_ref, o_ref):
    x = x_ref[...]
    a = _dot(x, wa_ref[...]) + ba_ref[...]
    g = _dot(x, wg_ref[...]) + bg_ref[...]
    o_ref[...] = a * jax.nn.sigmoid(g)


def glu_proj(x, w, b, tm=1024, tn=512):
    n, k = x.shape
    d = w.shape[1] // 2
    tm, tn = _pick(n, tm), _pick(d, tn)
    nj = d // tn
    b2 = b.reshape(1, 2 * d)
    return pl.pallas_call(
        _glu_kernel,
        grid=(n // tm, nj),
        in_specs=[pl.BlockSpec((tm, k), lambda i, j: (i, 0)),
                  pl.BlockSpec((k, tn), lambda i, j: (0, j)),
                  pl.BlockSpec((k, tn), lambda i, j: (0, j + nj)),
                  pl.BlockSpec((1, tn), lambda i, j: (0, j)),
                  pl.BlockSpec((1, tn), lambda i, j: (0, j + nj))],
        out_specs=pl.BlockSpec((tm, tn), lambda i, j: (i, j)),
        out_shape=jax.ShapeDtypeStruct((n, d), F32),
        compiler_params=_params(("parallel", "arbitrary")),
        name="glu_proj",
    )(x, w, w, b2, b2)


def _proj_res_ln_kernel(a_ref, w_ref, bias_ref, res_ref, g_ref, b_ref, o_ref, ob_ref):
    y = _dot(a_ref[...], w_ref[...]) + bias_ref[...]
    z = _layer_norm(DN_ALPHA * res_ref[...] + y, g_ref[...], b_ref[...])
    o_ref[...] = z
    ob_ref[...] = z.astype(BF16)


def proj_res_ln(a, w, bias, res, g, b, tm=512):
    n, k = a.shape
    d = w.shape[1]
    tm = _pick(n, tm)
    row = lambda i: (i, 0)
    fix = lambda i: (0, 0)
    return pl.pallas_call(
        _proj_res_ln_kernel,
        grid=(n // tm,),
        in_specs=[pl.BlockSpec((tm, k), row), pl.BlockSpec((k, d), fix),
                  pl.BlockSpec((1, d), fix), pl.BlockSpec((tm, d), row),
                  pl.BlockSpec((1, d), fix), pl.BlockSpec((1, d), fix)],
        out_specs=[pl.BlockSpec((tm, d), row), pl.BlockSpec((tm, d), row)],
        out_shape=[jax.ShapeDtypeStruct((n, d), F32), jax.ShapeDtypeStruct((n, d), BF16)],
        compiler_params=_params(("parallel",)),
        name="proj_res_ln",
    )(a, w, bias.reshape(1, d), res, g.reshape(1, d), b.reshape(1, d))


def _t5_bucket(rel):
    n = jnp.maximum(rel, 0)
    max_exact = REL_BUCKETS // 2
    nf = jnp.maximum(n, 1).astype(F32)
    large = max_exact + (jnp.log(nf / max_exact) / math.log(REL_MAX_DIST / max_exact)
                         * (REL_BUCKETS - max_exact)).astype(jnp.int32)
    large = jnp.minimum(large, REL_BUCKETS - 1)
    return jnp.where(n < max_exact, n, large)


def _bias_tile_kernel(rb_ref, o_ref, *, inv_scale):
    h = pl.program_id(0)
    d = pl.program_id(1)
    blk = o_ref.shape[-1]
    row = lax.broadcasted_iota(jnp.int32, (blk, blk), 0)
    col = lax.broadcasted_iota(jnp.int32, (blk, blk), 1)
    rel = d * blk + row - col
    bucket = _t5_bucket(rel)
    acc = jnp.zeros((blk, blk), F32)
    for k in range(REL_BUCKETS):
        acc = jnp.where(bucket == k, rb_ref[k, h], acc)
    o_ref[0, 0] = jnp.where(rel >= 0, acc * inv_scale, NEG_BIG)


def _num_bias_tiles(nb):
    last_start = (REL_MAX_DIST / (REL_BUCKETS // 2)) ** ((REL_BUCKETS // 2 - 1) / (REL_BUCKETS // 2)) \
        * (REL_BUCKETS // 2)
    d = 1
    while (d - 1) * MOBA_BLOCK + 1 < 1.05 * last_start + 1:
        d += 1
    return min(nb, d + 1)


def bias_tiles(rel_bias, n_heads, n_tiles, scale):
    return pl.pallas_call(
        functools.partial(_bias_tile_kernel, inv_scale=1.0 / scale),
        grid=(n_heads, n_tiles),
        in_specs=[pl.BlockSpec(memory_space=pltpu.SMEM)],
        out_specs=pl.BlockSpec((1, 1, MOBA_BLOCK, MOBA_BLOCK), lambda h, d: (h, d, 0, 0)),
        out_shape=jax.ShapeDtypeStruct((n_heads, n_tiles, MOBA_BLOCK, MOBA_BLOCK), F32),
        compiler_params=_params(("parallel", "parallel")),
        name="bias_tiles",
    )(rel_bias)


def _moba_kernel(q_ref, k_ref, v_ref, bias_ref, o_ref, kmean_sc, kaug_sc, *, nb, n_bias, scale):
    j = pl.program_id(2)
    blk = MOBA_BLOCK
    dh = q_ref.shape[2] // HEADS_STEP
    seq = k_ref.shape[1]
    c_exp = scale * math.log2(math.e)
    cw = KV_STEP * blk

    @pl.when(j == 0)
    def _():
        kmean_sc[...] = jnp.zeros_like(kmean_sc)
        for n in range(nb):
            kn = k_ref[0, n * blk:(n + 1) * blk, :].astype(F32)
            kmean_sc[n:n + 1, :] = jnp.mean(kn, axis=0, keepdims=True)
        blk_of_row = lax.broadcasted_iota(jnp.int32, (seq, LANES), 0) // blk
        blk_lane = lax.broadcasted_iota(jnp.int32, (seq, LANES), 1)
        onehot = jnp.where(blk_of_row == blk_lane, 1.0, 0.0).astype(BF16)
        for h in range(HEADS_STEP):
            kaug_sc[h, :, 0:dh] = k_ref[0, :, h * dh:(h + 1) * dh]
            kaug_sc[h, :, dh:dh + LANES] = onehot

    lane = lax.broadcasted_iota(jnp.int32, (blk, LANES), 1)
    lane_f = lane.astype(F32)
    j0 = pl.multiple_of(j * blk, blk)

    q_aug, state = [], []
    for h in range(HEADS_STEP):
        q = q_ref[0, :, h * dh:(h + 1) * dh]
        gate = _dot_nt(q, kmean_sc[:, h * dh:(h + 1) * dh].astype(BF16))
        gate = jnp.where(lane < j, gate, -jnp.inf)
        sel = jnp.zeros((blk, LANES), F32)
        for _ in range(MOBA_TOPK):
            mx = jnp.max(gate, axis=1, keepdims=True)
            first = jnp.min(jnp.where(gate == mx, lane_f, float(LANES)), axis=1, keepdims=True)
            hit = lane_f == first
            sel = jnp.where(hit, jnp.where(mx > -jnp.inf, 1.0, sel), sel)
            gate = jnp.where(hit, -jnp.inf, gate)
        selneg = jnp.where(sel > 0.0, 0.0, NEG_BIG).astype(BF16)
        q_aug.append(jnp.concatenate([q, selneg], axis=1))

        t = _dot_nt(q, k_ref[0, pl.ds(j0, blk), h * dh:(h + 1) * dh]) + bias_ref[h, 0]
        m0 = jnp.max(t, axis=1, keepdims=True)
        p = jnp.exp2((t - m0) * c_exp)
        l0 = jnp.sum(p, axis=1, keepdims=True)
        acc0 = _dot(p.astype(BF16), v_ref[0, pl.ds(j0, blk), h * dh:(h + 1) * dh])
        state.append((m0, l0, acc0))

    def body(c, carry):
        c0 = pl.multiple_of(c * cw, cw)
        out = []
        for h in range(HEADS_STEP):
            m, l, acc = carry[h]
            bias = jnp.concatenate(
                [bias_ref[h, jnp.clip(j - (c * KV_STEP + u), 0, n_bias - 1)] for u in range(KV_STEP)], axis=1)
            t = _dot_nt(q_aug[h], kaug_sc[h, pl.ds(c0, cw), :]) + bias
            m_new = jnp.maximum(m, jnp.max(t, axis=1, keepdims=True))
            alpha = jnp.exp2((m - m_new) * c_exp)
            p = jnp.exp2((t - m_new) * c_exp)
            l = alpha * l + jnp.sum(p, axis=1, keepdims=True)
            acc = alpha * acc + _dot(p.astype(BF16), v_ref[0, pl.ds(c0, cw), h * dh:(h + 1) * dh])
            out.append((m_new, l, acc))
        return tuple(out)

    state = lax.fori_loop(0, (j + KV_STEP - 1) // KV_STEP, body, tuple(state))
    o_ref[0] = jnp.concatenate([acc / l for (_, l, acc) in state], axis=1).astype(o_ref.dtype)


def moba_attention(qkv, tiles, batch, seq):
    d3 = qkv.shape[1]
    d = d3 // 3
    dh = d // N_HEADS
    nb = seq // MOBA_BLOCK
    assert nb % KV_STEP == 0 and N_HEADS % HEADS_STEP == 0 and dh == LANES
    n_bias = tiles.shape[1]
    hp = N_HEADS // HEADS_STEP
    wh = HEADS_STEP * dh
    qkv3 = qkv.reshape(batch, seq, d3)
    kern = functools.partial(_moba_kernel, nb=nb, n_bias=n_bias, scale=dh ** -0.5)
    out = pl.pallas_call(
        kern,
        grid=(batch, hp, nb),
        in_specs=[pl.BlockSpec((1, MOBA_BLOCK, wh), lambda b, h, j: (b, j, h)),
                  pl.BlockSpec((1, seq, wh), lambda b, h, j: (b, 0, hp + h)),
                  pl.BlockSpec((1, seq, wh), lambda b, h, j: (b, 0, 2 * hp + h)),
                  pl.BlockSpec((HEADS_STEP, n_bias, MOBA_BLOCK, MOBA_BLOCK), lambda b, h, j: (h, 0, 0, 0))],
        out_specs=pl.BlockSpec((1, MOBA_BLOCK, wh), lambda b, h, j: (b, j, h)),
        out_shape=jax.ShapeDtypeStruct((batch, seq, d), BF16),
        scratch_shapes=[pltpu.VMEM((LANES, wh), F32), pltpu.VMEM((HEADS_STEP, seq, dh + LANES), BF16)],
        compiler_params=_params(("parallel", "parallel", "arbitrary")),
        name="moba_attention",
    )(qkv3, qkv3, qkv3, tiles)
    return out.reshape(batch * seq, d)


def _dwconv_ln_kernel(prev_ref, cur_ref, w_ref, bdw_ref, g_ref, b_ref, o_ref, buf, conv_sc, hp_sc, *, halo):
    i = pl.program_id(1)
    ts, d = cur_ref.shape[1], cur_ref.shape[2]
    prev = prev_ref[0]
    buf[0:halo, :] = jnp.where(i > 0, prev, jnp.zeros_like(prev))
    buf[halo:halo + ts, :] = cur_ref[0]
    first = halo - (CONV_K - 1)
    for c in range(d // LANES):
        cs = slice(c * LANES, (c + 1) * LANES)
        acc = jnp.zeros((ts, LANES), F32) + bdw_ref[:, cs]
        for p in range(SUBLANES):
            span = (ts + halo - p) // SUBLANES * SUBLANES
            hp_sc[0:span, :] = buf[p:p + span, cs]
            for a in range(span // SUBLANES):
                k = a * SUBLANES + p - first
                if 0 <= k < CONV_K and a * SUBLANES + ts <= span:
                    acc = acc + hp_sc[a * SUBLANES:a * SUBLANES + ts, :] * w_ref[k:k + 1, cs]
        conv_sc[:, cs] = acc
    y = _layer_norm(conv_sc[...], g_ref[...], b_ref[...])
    o_ref[0] = (y * jax.nn.sigmoid(y)).astype(o_ref.dtype)


def dwconv_ln_silu(h, w_dw, b_dw, g, b, batch, seq, ts=256):
    d = h.shape[1]
    ts = _pick(seq, ts)
    halo = 32
    assert halo >= CONV_K - 1 and ts % halo == 0 and d % LANES == 0
    r = ts // halo
    h3 = h.reshape(batch, seq, d)
    fix = lambda bi, i: (0, 0)
    kern = functools.partial(_dwconv_ln_kernel, halo=halo)
    out = pl.pallas_call(
        kern,
        grid=(batch, seq // ts),
        in_specs=[pl.BlockSpec((1, halo, d), lambda bi, i: (bi, jnp.maximum(i * r - 1, 0), 0)),
                  pl.BlockSpec((1, ts, d), lambda bi, i: (bi, i, 0)),
                  pl.BlockSpec((CONV_K, d), fix), pl.BlockSpec((1, d), fix),
                  pl.BlockSpec((1, d), fix), pl.BlockSpec((1, d), fix)],
        out_specs=pl.BlockSpec((1, ts, d), lambda bi, i: (bi, i, 0)),
        out_shape=jax.ShapeDtypeStruct((batch, seq, d), BF16),
        scratch_shapes=[pltpu.VMEM((halo + ts, d), F32), pltpu.VMEM((ts, d), F32),
                        pltpu.VMEM((halo + ts, LANES), F32)],
        compiler_params=_params(("parallel", "parallel")),
        name="dwconv_ln_silu",
    )(h3, h3, w_dw, b_dw.reshape(1, d), g.reshape(1, d), b.reshape(1, d))
    return out.reshape(batch * seq, d)


def _router_kernel(x_ref, wr_ref, rb_ref, idx_ref, wt_ref, rank_ref, cnt_ref, carry, *, n_exp):
    step = pl.program_id(0)
    tm = x_ref.shape[0]
    per_g = n_exp // N_GROUPS
    assert per_g == SUBLANES
    logits = _dot_nt(wr_ref[...], x_ref[...])
    scores = jax.nn.sigmoid(logits)
    biased = scores + rb_ref[...]
    row_f = lax.broadcasted_iota(jnp.int32, (n_exp, tm), 0).astype(F32)

    def first_argmax(v, ids, none):
        mx = jnp.max(v, axis=0, keepdims=True)
        first = jnp.min(jnp.where(v == mx, ids, none), axis=0, keepdims=True)
        return mx, first

    grp_f = (lax.broadcasted_iota(jnp.int32, (n_exp, tm), 0) // per_g).astype(F32)
    gs = jnp.full((n_exp, tm), -jnp.inf, F32)
    for g in range(N_GROUPS):
        in_g = grp_f == float(g)
        vg = jnp.where(in_g, biased, -jnp.inf)
        m1, f1 = first_argmax(vg, row_f, float(n_exp))
        rest = jnp.where(row_f == f1, -jnp.inf, vg)
        m2 = jnp.max(rest, axis=0, keepdims=True)
        gs = jnp.where(in_g, jnp.maximum(vg, m1) + jnp.maximum(rest, m2), gs)
    keep = jnp.zeros((n_exp, tm), F32)
    for _ in range(TOPK_GROUPS):
        _, fg = first_argmax(gs, grp_f, float(N_GROUPS))
        hit = grp_f == fg
        keep = jnp.where(hit, 1.0, keep)
        gs = jnp.where(hit, -jnp.inf, gs)
    cand = jnp.where(keep > 0.0, biased, -jnp.inf)
    chosen = jnp.zeros((n_exp, tm), F32)
    picks, wks = [], []
    for k in range(TOP_K):
        _, fe = first_argmax(cand, row_f, float(n_exp))
        hit = row_f == fe
        wks.append(jnp.sum(jnp.where(hit, scores, 0.0), axis=0, keepdims=True))
        chosen = jnp.where(hit, 1.0, chosen)
        cand = jnp.where(hit, -jnp.inf, cand)
        picks.append(fe)
    wsum = wks[0]
    for k in range(1, TOP_K):
        wsum = wsum + wks[k]

    @pl.when(step == 0)
    def _():
        carry[...] = jnp.zeros_like(carry)

    r_i = lax.broadcasted_iota(jnp.int32, (tm, tm), 0)
    c_i = lax.broadcasted_iota(jnp.int32, (tm, tm), 1)
    earlier = jnp.where(r_i < c_i, 1.0, 0.0).astype(BF16)
    rank = _dot(chosen.astype(BF16), earlier) + carry[...]
    rks = [jnp.sum(jnp.where(row_f == picks[k], rank, 0.0), axis=0, keepdims=True) for k in range(TOP_K)]
    idx_ref[...] = jnp.concatenate(picks, axis=0).astype(jnp.int32)
    wt_ref[...] = jnp.concatenate(wks, axis=0) / wsum * ROUTED_SCALE
    rank_ref[...] = jnp.concatenate(rks, axis=0).astype(jnp.int32)
    total = carry[...] + jnp.broadcast_to(jnp.sum(chosen, axis=1, keepdims=True), chosen.shape)
    carry[...] = total
    cnt_ref[...] = total.astype(jnp.int32)


def router(x, w_router, r_bias, tm=256):
    n, d = x.shape
    n_exp = w_router.shape[1]
    assert n_exp <= LANES and n_exp % N_GROUPS == 0
    tm = _pick(n, tm)
    wr_t = w_router.T.astype(BF16)
    rb = jnp.broadcast_to(r_bias.astype(F32)[:, None], (n_exp, tm))
    col = lambda i: (0, i)
    fix = lambda i: (0, 0)
    idx_t, wts_t, rank_t, counts = pl.pallas_call(
        functools.partial(_router_kernel, n_exp=n_exp),
        grid=(n // tm,),
        in_specs=[pl.BlockSpec((tm, d), lambda i: (i, 0)), pl.BlockSpec((n_exp, d), fix),
                  pl.BlockSpec((n_exp, tm), fix)],
        out_specs=[pl.BlockSpec((TOP_K, tm), col), pl.BlockSpec((TOP_K, tm), col),
                   pl.BlockSpec((TOP_K, tm), col), pl.BlockSpec((n_exp, tm), fix)],
        out_shape=[jax.ShapeDtypeStruct((TOP_K, n), jnp.int32), jax.ShapeDtypeStruct((TOP_K, n), F32),
                   jax.ShapeDtypeStruct((TOP_K, n), jnp.int32), jax.ShapeDtypeStruct((n_exp, tm), jnp.int32)],
        scratch_shapes=[pltpu.VMEM((n_exp, tm), F32)],
        compiler_params=_params(("arbitrary",)),
        name="router",
    )(x, wr_t, rb)
    return idx_t.T, wts_t.T, rank_t.T, counts[:, 0]


def expert_plan(idx, rank, counts, n_exp, tm):
    n_tok = idx.shape[0]
    n_asg = n_tok * TOP_K
    n_tiles = n_asg // tm
    n_items = n_tiles + n_exp - 1
    ends = jnp.cumsum(counts)
    starts = ends - counts
    first_blk = starts // tm
    n_items_e = jnp.where(counts > 0, (ends - 1) // tm - first_blk + 1, 0)
    item_end = jnp.cumsum(n_items_e)
    item_start = item_end - n_items_e
    n_live = item_end[-1]

    e_ids = jnp.arange(n_exp, dtype=jnp.int32)
    mine = idx[:, :, None] == e_ids
    lookup = lambda table: jnp.sum(jnp.where(mine, table, 0), axis=-1)
    pos = rank + lookup(starts)
    pos_out = (lookup(item_start - first_blk) + pos // tm) * tm + pos % tm

    it = jnp.minimum(jnp.arange(n_items, dtype=jnp.int32), n_live - 1)
    onehot = (jnp.sum(item_end[None, :] <= it[:, None], axis=1)[:, None] == e_ids).astype(jnp.int32)
    pick = lambda v: jnp.sum(onehot * v[None, :], axis=1)
    item_exp = pick(e_ids)
    item_blk = jnp.clip(pick(first_blk) + it - pick(item_start), 0, n_tiles - 1)
    new_exp = jnp.concatenate([jnp.ones((1,), jnp.int32), (item_exp[1:] != item_exp[:-1]).astype(jnp.int32)])
    tables = tuple(t.astype(jnp.int32) for t in (item_blk, item_exp, new_exp, n_live.reshape(1)))
    return pos.astype(jnp.int32), pos_out.astype(jnp.int32), tables


def _dispatch_kernel(pos_ref, x_ref, xs_hbm, sem):
    tt = x_ref.shape[0]
    for t in range(tt):
        for k in range(TOP_K):
            pltpu.make_async_copy(x_ref.at[pl.ds(t, 1)], xs_hbm.at[pl.ds(pos_ref[0, 0, t * TOP_K + k], 1)],
                                  sem).start(priority=k % 2)
    for _ in range(TOP_K):
        pltpu.make_async_copy(x_ref, xs_hbm.at[pl.ds(0, tt)], sem).wait()


def dispatch(x, pos, tt=128):
    n, d = x.shape
    tt = _pick(n, tt)
    pos3 = pos.reshape(n // tt, 1, tt * TOP_K)
    return pl.pallas_call(
        _dispatch_kernel,
        grid=(n // tt,),
        in_specs=[pl.BlockSpec((1, 1, tt * TOP_K), lambda i: (i, 0, 0), memory_space=pltpu.SMEM),
                  pl.BlockSpec((tt, d), lambda i: (i, 0))],
        out_specs=pl.BlockSpec(memory_space=pl.ANY),
        out_shape=jax.ShapeDtypeStruct((n * TOP_K, d), x.dtype),
        scratch_shapes=[pltpu.SemaphoreType.DMA],
        compiler_params=_params(("arbitrary",)),
        name="dispatch",
    )(pos3, x)


def _experts_kernel(blk_ref, exp_ref, newexp_ref, nlive_ref, xs_ref, wg_ref, wu_ref, wd_ref, y_ref, wgb, wub, wdb):
    i = pl.program_id(0)

    @pl.when(i < nlive_ref[0])
    def _():
        @pl.when(newexp_ref[i] == 1)
        def _():
            wgb[...] = wg_ref[...].astype(BF16)
            wub[...] = wu_ref[...].astype(BF16)
            wdb[...] = wd_ref[...].astype(BF16)

        x = xs_ref[...].astype(BF16)
        g = _dot(x, wgb[...])
        u = _dot(x, wub[...])
        h = (g * jax.nn.sigmoid(g) * u).astype(BF16)
        y_ref[...] = _dot(h, wdb[...])

    @pl.when(i >= nlive_ref[0])
    def _():
        y_ref[...] = jnp.zeros_like(y_ref)


def routed_experts(xs, tables, w_gate, w_up, w_down, layer, tm):
    n_asg, d = xs.shape
    n_exp, f = w_gate.shape[1], w_gate.shape[3]
    n_items = n_asg // tm + n_exp - 1
    w_in = pl.BlockSpec((None, None, d, f), lambda i, blk, ex, *_: (layer, ex[i], 0, 0))
    w_out = pl.BlockSpec((None, None, f, d), lambda i, blk, ex, *_: (layer, ex[i], 0, 0))
    grid_spec = pltpu.PrefetchScalarGridSpec(
        num_scalar_prefetch=4,
        grid=(n_items,),
        in_specs=[pl.BlockSpec((tm, d), lambda i, blk, *_: (blk[i], 0)), w_in, w_in, w_out],
        out_specs=pl.BlockSpec((tm, d), lambda i, *_: (i, 0)),
        scratch_shapes=[pltpu.VMEM((d, f), BF16), pltpu.VMEM((d, f), BF16), pltpu.VMEM((f, d), BF16)],
    )
    return pl.pallas_call(
        _experts_kernel,
        grid_spec=grid_spec,
        out_shape=jax.ShapeDtypeStruct((n_items * tm, d), F32),
        compiler_params=_params(("arbitrary",)),
        name="routed_experts",
    )(*tables, xs, w_gate, w_up, w_down)


def _combine_kernel(pos_ref, posn_ref, y_hbm, wt_ref, xb_ref, sg_ref, su_ref, sd_ref, res_ref, g_ref, b_ref,
                    o_ref, ob_ref, ybuf, sems):
    i = pl.program_id(0)
    n_steps = pl.num_programs(0)
    tc = res_ref.shape[0]

    def gather_wait(s):
        for k in range(TOP_K):
            pltpu.make_async_copy(y_hbm.at[pl.ds(0, tc)], ybuf.at[s, k], sems.at[s]).wait()

    @pl.when(i == 0)
    def _():
        def token(t, c):
            for k in range(TOP_K):
                pltpu.make_async_copy(y_hbm.at[pl.ds(pos_ref[0, 0, t * TOP_K + k], 1)],
                                      ybuf.at[0, k, pl.ds(t, 1)], sems.at[0]).start(priority=k % 2)
            return c
        lax.fori_loop(0, tc, token, 0)

    def step(slot):
        nxt = 1 - slot
        for t in range(tc):
            for k in range(TOP_K):
                pltpu.make_async_copy(y_hbm.at[pl.ds(posn_ref[0, 0, t * TOP_K + k], 1)],
                                      ybuf.at[nxt, k, pl.ds(t, 1)], sems.at[nxt]).start(priority=k % 2)

        xb = xb_ref[...]
        hg = _dot(xb, sg_ref[...])
        hu = _dot(xb, su_ref[...])
        shared = _dot((hg * jax.nn.sigmoid(hg) * hu).astype(BF16), sd_ref[...])
        base = DN_ALPHA * res_ref[...] + shared

        gather_wait(slot)
        wts = wt_ref[...]
        routed = ybuf[slot, 0] * wts[:, 0:1]
        for k in range(1, TOP_K):
            routed = routed + ybuf[slot, k] * wts[:, k:k + 1]
        z = _layer_norm(base + routed, g_ref[...], b_ref[...])
        o_ref[...] = z
        ob_ref[...] = z.astype(BF16)

        @pl.when(i == n_steps - 1)
        def _():
            gather_wait(nxt)

    for slot in range(2):
        pl.when(i % 2 == slot)(functools.partial(step, slot))


def combine_shared_ln(y, pos, wts, xb, s_gate, s_up, s_down, res, g, b, tc=128):
    n, d = res.shape
    f = s_gate.shape[1]
    tc = _pick(n, tc)
    n_steps = n // tc
    pos3 = pos.reshape(n_steps, 1, tc * TOP_K)
    row = lambda i: (i, 0)
    fix = lambda i: (0, 0)
    pos_blk = lambda imap: pl.BlockSpec((1, 1, tc * TOP_K), imap, memory_space=pltpu.SMEM)
    return pl.pallas_call(
        _combine_kernel,
        grid=(n_steps,),
        in_specs=[pos_blk(lambda i: (i, 0, 0)),
                  pos_blk(lambda i: (jnp.minimum(i + 1, n_steps - 1), 0, 0)),
                  pl.BlockSpec(memory_space=pl.ANY),
                  pl.BlockSpec((tc, LANES), row), pl.BlockSpec((tc, d), row),
                  pl.BlockSpec((d, f), fix), pl.BlockSpec((d, f), fix), pl.BlockSpec((f, d), fix),
                  pl.BlockSpec((tc, d), row), pl.BlockSpec((1, d), fix), pl.BlockSpec((1, d), fix)],
        out_specs=[pl.BlockSpec((tc, d), row), pl.BlockSpec((tc, d), row)],
        out_shape=[jax.ShapeDtypeStruct((n, d), F32), jax.ShapeDtypeStruct((n, d), BF16)],
        scratch_shapes=[pltpu.VMEM((2, TOP_K, tc, d), F32), pltpu.SemaphoreType.DMA((2,))],
        compiler_params=_params(("arbitrary",)),
        name="combine_shared_ln",
    )(pos3, pos3, y, wts, xb, s_gate, s_up, s_down, res, g.reshape(1, d), b.reshape(1, d))


def moe_ffn_ln(xf, xb, layer, w_router, r_bias, w_gate, w_up, w_down, s_gate, s_up, s_down, g, b, tm=512):
    n_exp = w_router.shape[1]
    tm = _pick(xf.shape[0] * TOP_K, tm)
    idx, wts, rank, counts = router(xb, w_router, r_bias)
    pos, pos_out, tables = expert_plan(idx, rank, counts, n_exp, tm)
    xs = dispatch(xf, pos)
    y = routed_experts(xs, tables, w_gate, w_up, w_down, layer, tm)
    return combine_shared_ln(y, pos_out, jnp.pad(wts, ((0, 0), (0, LANES - TOP_K))), xb, s_gate.astype(BF16), s_up.astype(BF16), s_down.astype(BF16),
                             xf, g, b)


def kernel(x, rel_bias, attn_w_qkv, attn_w_o, conv_w_in, conv_b_in, conv_w_dw, conv_b_dw,
           conv_ln_g, conv_ln_b, conv_w_out, conv_b_out, ln_mix_g, ln_mix_b, ln_ffn_g, ln_ffn_b,
           moe_w_router, moe_router_bias, moe_w_gate, moe_w_up, moe_w_down,
           shared_w_gate, shared_w_up, shared_w_down):
    batch, seq, d = x.shape
    n = batch * seq
    depth = ln_mix_g.shape[0]
    nb = seq // MOBA_BLOCK
    assert seq % MOBA_BLOCK == 0 and d % N_HEADS == 0 and depth == DEPTH

    tiles = bias_tiles(rel_bias, N_HEADS, _num_bias_tiles(nb), (d // N_HEADS) ** -0.5)
    xf = x.reshape(n, d)
    xb = xf.astype(BF16)
    zero_bias = jnp.zeros((d,), F32)
    for i in range(depth):
        m = i // 2
        if i % 2 == 0:
            qkv = matmul(xb, attn_w_qkv[m].astype(BF16), BF16)
            a = moba_attention(qkv, tiles, batch, seq)
            xf, xb = proj_res_ln(a, attn_w_o[m].astype(BF16), zero_bias, xf, ln_mix_g[i], ln_mix_b[i])
        else:
            h = glu_proj(xb, conv_w_in[m].astype(BF16), conv_b_in[m])
            a = dwconv_ln_silu(h, conv_w_dw[m], conv_b_dw[m], conv_ln_g[m], conv_ln_b[m], batch, seq)
            xf, xb = proj_res_ln(a, conv_w_out[m].astype(BF16), conv_b_out[m], xf, ln_mix_g[i], ln_mix_b[i])
        xf, xb = moe_ffn_ln(xf, xb, i, moe_w_router[i], moe_router_bias[i], moe_w_gate, moe_w_up, moe_w_down,
                            shared_w_gate[i], shared_w_up[i], shared_w_down[i], ln_ffn_g[i], ln_ffn_b[i])
    return xf.reshape(batch, seq, d)
```
